```python
import math
import jax
import jax.numpy as jnp
from jax import lax
import numpy as np

D_MODEL = 1024
BATCH = 16
SEQ = 4096
DEPTH = 1
DEC_BATCH = 8
DEC_SEQ = 64
PAST_LEN = 4096

CHUNK = 64
N_META = 16
EPS = 1e-6
MLA_HEADS = 8
Q_LORA = 384
KV_LORA = 256
NOPE_DIM = 64
ROPE_DIM = 32
QK_HEAD = NOPE_DIM + ROPE_DIM
V_HEAD = 64
MLA_WIDTH = MLA_HEADS * V_HEAD
ROPE_BASE = 10000.0
Q_BLOCK = 128
GDN_HEADS = 4
GDN_DK = 128
GDN_DV = 128
GDN_QK_WIDTH = GDN_HEADS * GDN_DK
GDN_V_WIDTH = GDN_HEADS * GDN_DV
CONV_W = 4
CONV_CH = 2 * GDN_QK_WIDTH + GDN_V_WIDTH
GDN_CHUNK = 64
MIX_WIDTH = MLA_WIDTH + GDN_V_WIDTH
D_FF = 4 * D_MODEL
SPLIT_Q = Q_LORA
SPLIT_KV = SPLIT_Q + KV_LORA
SPLIT_ROPE = SPLIT_KV + ROPE_DIM
SPLIT_QKV = SPLIT_ROPE + CONV_CH
SPLIT_GATE = SPLIT_QKV + GDN_V_WIDTH
SPLIT_BETA = SPLIT_GATE + GDN_HEADS
IN_COLS = SPLIT_BETA + GDN_HEADS
IN_SPLITS = (SPLIT_Q, SPLIT_KV, SPLIT_ROPE, SPLIT_QKV, SPLIT_GATE, SPLIT_BETA)

kernel_name = 'hymba_mla_gdn_streaming_step'


def rms_norm(x, g):
    xf = x.astype(jnp.float32)
    y = xf * lax.rsqrt(jnp.mean(xf * xf, axis=-1, keepdims=True) + EPS)
    return (y * g.astype(jnp.float32)).astype(x.dtype)


def l2_norm(x):
    xf = x.astype(jnp.float32)
    return (xf * lax.rsqrt(jnp.sum(xf * xf, axis=-1, keepdims=True) + EPS)).astype(x.dtype)


def apply_rope(x, pos):
    half = ROPE_DIM // 2
    inv_freq = ROPE_BASE ** (-jnp.arange(half, dtype=jnp.float32) / half)
    ang = pos.astype(jnp.float32)[:, None] * inv_freq[None, :]
    cos = jnp.cos(ang)[None, :, None, :]
    sin = jnp.sin(ang)[None, :, None, :]
    xf = x.astype(jnp.float32)
    x1, x2 = xf[..., :half], xf[..., half:]
    return jnp.concatenate([x1 * cos - x2 * sin, x2 * cos + x1 * sin], axis=-1).astype(x.dtype)


def input_projection(x, pos, lp):
    h = rms_norm(x, lp['attn_norm'])
    z = jnp.einsum('bld,dc->blc', h, lp['w_in'])
    q_lat, kv_lat, k_rope_raw, qkv, gate, beta_logit, a_logit = jnp.split(z, IN_SPLITS, axis=-1)
    w_uq = lp['w_uq'].reshape(Q_LORA, MLA_HEADS, QK_HEAD)
    q = jnp.einsum('blr,rhd->blhd', rms_norm(q_lat, lp['q_a_norm']), w_uq)
    q = rms_norm(q, lp['q_norm'])
    q = jnp.concatenate([q[..., :NOPE_DIM], apply_rope(q[..., NOPE_DIM:], pos)], axis=-1)
    kv_lat = rms_norm(kv_lat, lp['kv_a_norm'])
    return q, kv_lat, k_rope_raw, qkv, gate, beta_logit, a_logit


def mla_keys_values(kv_lat, k_rope_raw, pos, lp):
    B, L, _ = kv_lat.shape
    k_nope = jnp.einsum('blr,rhd->blhd', kv_lat, lp['w_uk'].reshape(KV_LORA, MLA_HEADS, NOPE_DIM))
    v = jnp.einsum('blr,rhd->blhd', kv_lat, lp['w_uv'].reshape(KV_LORA, MLA_HEADS, V_HEAD))
    k_rope = jnp.broadcast_to(k_rope_raw[:, :, None, :], (B, L, MLA_HEADS, ROPE_DIM))
    k = rms_norm(jnp.concatenate([k_nope, k_rope], axis=-1), lp['k_norm'])
    k = jnp.concatenate([k[..., :NOPE_DIM], apply_rope(k[..., NOPE_DIM:], pos)], axis=-1)
    return k, v


def attend(q, k, v, mask):
    s = jnp.einsum('bqhd,bkhd->bhqk', q, k).astype(jnp.float32) * (QK_HEAD ** -0.5)
    if mask is not None:
        s = jnp.where(mask, s, -jnp.inf)
    p = jax.nn.softmax(s, axis=-1).astype(v.dtype)
    return jnp.einsum('bhqk,bkhd->bqhd', p, v)


def chunk_causal_attention(q, k, v, chunk_id):
    B, L, H, D = q.shape
    nb = -(-L // Q_BLOCK)
    pad = nb * Q_BLOCK - L
    qb = jnp.pad(q, ((0, 0), (0, pad), (0, 0), (0, 0))).reshape(B, nb, Q_BLOCK, H, D).transpose(1, 0, 2, 3, 4)
    cb = jnp.pad(chunk_id, (0, pad), constant_values=L).reshape(nb, Q_BLOCK)

    def one_block(args):
        q_blk, c_blk = args
        return attend(q_blk, k, v, c_blk[:, None] >= chunk_id[None, :])

    o = lax.map(one_block, (qb, cb))
    return o.transpose(1, 0, 2, 3, 4).reshape(B, nb * Q_BLOCK, H, V_HEAD)[:, :L]


def causal_conv_silu(u, buf, w):
    xp = jnp.concatenate([buf.astype(u.dtype), u], axis=1)
    y = lax.conv_general_dilated(xp, w.astype(u.dtype)[:, None, :], window_strides=(1,), padding='VALID',
                                 dimension_numbers=('NWC', 'WIO', 'NWC'), feature_group_count=CONV_CH)
    return jax.nn.silu(y), xp[:, -(CONV_W - 1):]


def gated_delta_chunked(q, k, v, g, beta, state0):
    B, L, H, _ = q.shape
    C = GDN_CHUNK
    n = -(-L // C)
    pad = n * C - L

    def blocks(t):
        t = jnp.pad(t.astype(jnp.float32), ((0, 0), (0, pad)) + ((0, 0),) * (t.ndim - 2))
        t = t.reshape((B, n, C) + t.shape[2:])
        return t.transpose((1, 0, 3, 2, 4)[:t.ndim])

    qc, kc, vc = blocks(q), blocks(k), blocks(v)
    gc = jnp.cumsum(blocks(g), axis=-1)
    bc = blocks(beta)
    idx = jnp.arange(C)
    causal = idx[:, None] >= idx[None, :]
    strict = idx[:, None] > idx[None, :]
    decay = jnp.exp(jnp.where(causal, gc[..., :, None] - gc[..., None, :], -jnp.inf))
    kb = kc * bc[..., None]
    a = jnp.where(strict, jnp.einsum('...id,...jd->...ij', kb, kc) * decay, 0.0)
    eye = jnp.eye(C, dtype=jnp.float32)
    t_inv = lax.linalg.triangular_solve(eye + a, jnp.broadcast_to(eye, a.shape), left_side=True, lower=True)
    u = t_inv @ (vc * bc[..., None])
    w = t_inv @ (kb * jnp.exp(gc)[..., None])
    qk = jnp.einsum('...id,...jd->...ij', qc, kc) * decay

    def step(S, xs):
        q_i, k_i, u_i, w_i, qk_i, g_i = xs
        v_new = u_i - w_i @ S
        o = (q_i * jnp.exp(g_i)[..., None]) @ S + qk_i @ v_new
        g_last = g_i[..., -1:]
        S = S * jnp.exp(g_last)[..., None] + jnp.einsum('bhck,bhcv->bhkv', k_i * jnp.exp(g_last - g_i)[..., None], v_new)
        return S, o

    S, o = lax.scan(step, state0.astype(jnp.float32), (qc, kc, u, w, qk, gc))
    o = o.transpose(1, 0, 3, 2, 4).reshape(B, n * C, H, GDN_DV)[:, :L]
    return o.astype(v.dtype), S.astype(state0.dtype)


def gdn_branch(qkv, gate, beta_logit, a_logit, conv_buf, state0, lp):
    B, L, _ = qkv.shape
    conv_out, new_buf = causal_conv_silu(qkv, conv_buf, lp['conv_w'])
    q, k, v = jnp.split(conv_out, (GDN_QK_WIDTH, 2 * GDN_QK_WIDTH), axis=-1)
    q = l2_norm(q.reshape(B, L, GDN_HEADS, GDN_DK)) * (GDN_DK ** -0.5)
    k = l2_norm(k.reshape(B, L, GDN_HEADS, GDN_DK))
    v = v.reshape(B, L, GDN_HEADS, GDN_DV)
    beta = jax.nn.sigmoid(beta_logit.astype(jnp.float32))
    g = -jnp.exp(lp['a_log'].astype(jnp.float32)) * jax.nn.softplus(a_logit.astype(jnp.float32) + lp['dt_bias'].astype(jnp.float32))
    o, S = gated_delta_chunked(q, k, v, g, beta, state0)
    o = rms_norm(o, lp['gdn_out_norm']) * jax.nn.silu(gate.reshape(B, L, GDN_HEADS, GDN_DV))
    return o.reshape(B, L, GDN_V_WIDTH), S, new_buf


def finish_layer(x, attn, gdn, lp):
    B, L, _ = x.shape
    mix = jnp.concatenate([rms_norm(attn.reshape(B, L, MLA_WIDTH), lp['mla_out_norm']), gdn], axis=-1)
    x = x + jnp.einsum('blc,cd->bld', mix, lp['w_out'])
    h = rms_norm(x, lp['mlp_norm'])
    up = jnp.square(jax.nn.relu(jnp.einsum('bld,df->blf', h, lp['w_up'])))
    return x + jnp.einsum('blf,fd->bld', up, lp['w_down'])


def setup_inputs(seed: int = 0) -> dict:
    key = jax.random.key(seed)
    keys = jax.random.split(key, 28)
    f32 = jnp.float32

    def normal(i, shape, scale):
        return jax.random.normal(keys[i], shape, f32) * scale

    def gain(i, shape):
        return 1.0 + 0.01 * jax.random.normal(keys[i], shape, f32)

    dt = jnp.exp(jax.random.uniform(keys[24], (DEPTH, GDN_HEADS), f32, math.log(1e-3), math.log(1e-1)))
    return {
        'x_prompt': normal(0, (BATCH, SEQ, D_MODEL), 1.0),
        'x_sample': normal(1, (DEC_BATCH, DEC_SEQ, D_MODEL), 1.0),
        'cache_kv_latent': normal(2, (DEPTH, DEC_BATCH, N_META + PAST_LEN, KV_LORA), 1.0),
        'cache_k_rope': normal(3, (DEPTH, DEC_BATCH, N_META + PAST_LEN, ROPE_DIM), 1.0),
        'state_gdn': normal(4, (DEPTH, DEC_BATCH, GDN_HEADS, GDN_DK, GDN_DV), 0.5),
        'state_conv': normal(5, (DEPTH, DEC_BATCH, CONV_W - 1, CONV_CH), 1.0),
        'meta_tokens': normal(6, (N_META, D_MODEL), 1.0),
        'attn_norm': gain(7, (DEPTH, D_MODEL)),
        'w_in': normal(8, (DEPTH, D_MODEL, IN_COLS), D_MODEL ** -0.5),
        'q_a_norm': gain(9, (DEPTH, Q_LORA)),
        'w_uq': normal(10, (DEPTH, Q_LORA, MLA_HEADS * QK_HEAD), Q_LORA ** -0.5),
        'kv_a_norm': gain(11, (DEPTH, KV_LORA)),
        'w_uk': normal(12, (DEPTH, KV_LORA, MLA_HEADS * NOPE_DIM), KV_LORA ** -0.5),
        'w_uv': normal(13, (DEPTH, KV_LORA, MLA_HEADS * V_HEAD), KV_LORA ** -0.5),
        'q_norm': gain(14, (DEPTH, QK_HEAD)),
        'k_norm': gain(15, (DEPTH, QK_HEAD)),
        'mla_out_norm': gain(16, (DEPTH, MLA_WIDTH)),
        'conv_w': normal(17, (DEPTH, CONV_W, CONV_CH), CONV_W ** -0.5),
        'a_log': jnp.log(jax.random.uniform(keys[18], (DEPTH, GDN_HEADS), f32, 1.0, 16.0)),
        'dt_bias': dt + jnp.log(-jnp.expm1(-dt)),
        'gdn_out_norm': gain(19, (DEPTH, GDN_DV)),
        'w_out': normal(20, (DEPTH, MIX_WIDTH, D_MODEL), MIX_WIDTH ** -0.5),
        'mlp_norm': gain(21, (DEPTH, D_MODEL)),
        'w_up': normal(22, (DEPTH, D_MODEL, D_FF), D_MODEL ** -0.5),
        'w_down': normal(23, (DEPTH, D_FF, D_MODEL), D_FF ** -0.5),
    }


def reference(x_prompt, x_sample, cache_kv_latent, cache_k_rope, state_gdn, state_conv,
              meta_tokens, attn_norm, w_in, q_a_norm, w_uq, kv_a_norm, w_uk, w_uv, q_norm, k_norm,
              mla_out_norm, conv_w, a_log, dt_bias, gdn_out_norm, w_out, mlp_norm, w_up, w_down):
    B = x_prompt.shape[0]
    xp = jnp.concatenate([jnp.broadcast_to(meta_tokens.astype(x_prompt.dtype)[None], (B, N_META, D_MODEL)), x_prompt], axis=1)
    Lp = xp.shape[1]
    pos_p = jnp.arange(Lp)
    frame_p = pos_p - N_META
    chunk_p = jnp.where(frame_p < 0, -1, frame_p // CHUNK)
    xs = x_sample
    Bs, Ls = xs.shape[:2]
    Lc = cache_kv_latent.shape[2]
    pos_s = Lc + jnp.arange(Ls)
    pos_cs = jnp.arange(Lc + Ls)
    p_lat, p_rope, p_S, p_buf, s_lat, s_rope, s_S, s_buf = [], [], [], [], [], [], [], []
    for l in range(DEPTH):
        lp = {'attn_norm': attn_norm[l], 'w_in': w_in[l], 'q_a_norm': q_a_norm[l], 'w_uq': w_uq[l],
              'kv_a_norm': kv_a_norm[l], 'w_uk': w_uk[l], 'w_uv': w_uv[l], 'q_norm': q_norm[l],
              'k_norm': k_norm[l], 'mla_out_norm': mla_out_norm[l], 'conv_w': conv_w[l], 'a_log': a_log[l],
              'dt_bias': dt_bias[l], 'gdn_out_norm': gdn_out_norm[l], 'w_out': w_out[l],
              'mlp_norm': mlp_norm[l], 'w_up': w_up[l], 'w_down': w_down[l]}
        q, kv_lat, k_rope_raw, qkv, gate, bl, al = input_projection(xp, pos_p, lp)
        k, v = mla_keys_values(kv_lat, k_rope_raw, pos_p, lp)
        attn = chunk_causal_attention(q, k, v, chunk_p)
        gdn, S_new, buf_new = gdn_branch(qkv, gate, bl, al,
                                         jnp.zeros((B, CONV_W - 1, CONV_CH), xp.dtype),
                                         jnp.zeros((B, GDN_HEADS, GDN_DK, GDN_DV), xp.dtype), lp)
        xp = finish_layer(xp, attn, gdn, lp)
        p_lat.append(kv_lat)
        p_rope.append(k_rope_raw)
        p_S.append(S_new)
        p_buf.append(buf_new)
        q, kv_lat, k_rope_raw, qkv, gate, bl, al = input_projection(xs, pos_s, lp)
        lat_all = jnp.concatenate([cache_kv_latent[l].astype(kv_lat.dtype), kv_lat], axis=1)
        rope_all = jnp.concatenate([cache_k_rope[l].astype(k_rope_raw.dtype), k_rope_raw], axis=1)
        k, v = mla_keys_values(lat_all, rope_all, pos_cs, lp)
        attn = attend(q, k, v, None)
        gdn, S_new, buf_new = gdn_branch(qkv, gate, bl, al, state_conv[l], state_gdn[l], lp)
        xs = finish_layer(xs, attn, gdn, lp)
        s_lat.append(kv_lat)
        s_rope.append(k_rope_raw)
        s_S.append(S_new)
        s_buf.append(buf_new)
    return (xp[:, N_META:], xs, jnp.stack(p_lat), jnp.stack(p_rope), jnp.stack(p_S), jnp.stack(p_buf),
            jnp.stack(s_lat), jnp.stack(s_rope), jnp.stack(s_S), jnp.stack(s_buf))
```

```python
import functools
import math

import jax
import jax.numpy as jnp
import numpy as np
from jax import lax
from jax.experimental import pallas as pl
from jax.experimental.pallas import tpu as pltpu

F32 = jnp.float32
BF16 = jnp.bfloat16

D_MODEL = 1024
N_META = 16
CHUNK = 64
EPS = 1e-6
MLA_HEADS = 8
Q_LORA = 384
KV_LORA = 256
NOPE_DIM = 64
ROPE_DIM = 32
QK_HEAD = NOPE_DIM + ROPE_DIM
V_HEAD = 64
MLA_WIDTH = MLA_HEADS * V_HEAD
ROPE_BASE = 10000.0
GDN_HEADS = 4
GDN_DK = 128
GDN_DV = 128
GDN_WIDTH = GDN_HEADS * GDN_DK
CONV_W = 4
CONV_CH = 3 * GDN_WIDTH
D_FF = 4 * D_MODEL

LANES = 128
SUBLANES = 8
QK_WIDTH = MLA_HEADS * LANES
HALF_ROPE = ROPE_DIM // 2
VMEM_LIMIT = 56 * 1024 * 1024

COL_Q = 0
COL_KV = COL_Q + Q_LORA
COL_KR = COL_KV + KV_LORA
COL_U = COL_KR + LANES
COL_GATE = COL_U + CONV_CH
COL_LOGIT = COL_GATE + GDN_WIDTH
IN_COLS_PAD = COL_LOGIT + LANES


def _head_lane_source():
    src = [-1] * LANES
    for d in range(HALF_ROPE):
        src[d] = NOPE_DIM + d
        src[64 + d] = NOPE_DIM + HALF_ROPE + d
    for d in range(48):
        src[16 + d] = d
    for d in range(16):
        src[80 + d] = 48 + d
    return src


_SRC = _head_lane_source()
_SRC_IDX = np.array([max(s, 0) for s in _SRC], np.int32)
_SRC_ANY = np.array([1.0 if s >= 0 else 0.0 for s in _SRC], np.float32)
_SRC_NOPE = np.array([1.0 if 0 <= s < NOPE_DIM else 0.0 for s in _SRC], np.float32)


def _mm(a, b):
    return jnp.dot(a.astype(BF16), b.astype(BF16), preferred_element_type=F32)


def _mm_nt(a, b):
    return lax.dot_general(a.astype(BF16), b.astype(BF16), (((1,), (1,)), ((), ())),
                           preferred_element_type=F32)


def _mm_tn(a, b):
    return lax.dot_general(a.astype(BF16), b.astype(BF16), (((0,), (0,)), ((), ())),
                           preferred_element_type=F32)


def _rms(x, g):
    return x * lax.rsqrt(jnp.mean(x * x, axis=-1, keepdims=True) + EPS) * g


def _sigmoid(x):
    return 1.0 / (1.0 + jnp.exp(-x))


def _row_tile(n, cap):
    t = cap
    while t > 1 and n % t:
        t //= 2
    return t


def _params(*sem):
    return pltpu.CompilerParams(dimension_semantics=sem, vmem_limit_bytes=VMEM_LIMIT)


def _const_spec(shape):
    nd = len(shape)
    return pl.BlockSpec(shape, lambda *_: (0,) * nd, pipeline_mode=pl.Buffered(1))


def _head_norm_rope(raw, extra, c_tab, s_tab, out_ref):
    for h in range(MLA_HEADS):
        seg = raw[:, h * LANES:(h + 1) * LANES]
        if extra is not None:
            seg = seg + extra
        r = lax.rsqrt(jnp.sum(seg * seg, axis=-1, keepdims=True) * (1.0 / QK_HEAD) + EPS)
        out = (seg * c_tab + pltpu.roll(seg, 64, 1) * s_tab) * r
        out_ref[:, h * LANES:(h + 1) * LANES] = out.astype(out_ref.dtype)


def _in_proj_kernel(x_ref, g_attn_ref, w_in_ref, g_qa_ref, w_uq_ref, g_kv_ref, cq_ref, sq_ref,
                    q_ref, lat_ref, krp_ref, u_ref, gate_ref, logit_ref, tail_ref):
    x = x_ref[...]
    hb = _rms(x, g_attn_ref[...]).astype(BF16)

    def proj(lo, hi):
        return jnp.dot(hb, w_in_ref[:, lo:hi], preferred_element_type=F32)

    q_lat = _rms(proj(COL_Q, COL_KV), g_qa_ref[...])
    q_raw = _mm(q_lat, w_uq_ref[...])
    _head_norm_rope(q_raw, None, cq_ref[...], sq_ref[...], q_ref)
    lat_ref[...] = _rms(proj(COL_KV, COL_KR), g_kv_ref[...])
    krp_ref[...] = proj(COL_KR, COL_U)
    u = proj(COL_U, COL_GATE)
    u_ref[...] = u.astype(BF16)
    tail_ref[...] = u[u.shape[0] - SUBLANES:, :]
    gate_ref[...] = proj(COL_GATE, COL_LOGIT).astype(BF16)
    logit_ref[...] = proj(COL_LOGIT, IN_COLS_PAD)


def _in_proj(x, cq, sq, wp):
    nb, n, _ = x.shape
    tm = _row_tile(n, 512)
    grid = (nb, n // tm)
    row = lambda w: pl.BlockSpec((None, tm, w), lambda b, i: (b, i, 0))
    tab = pl.BlockSpec((tm, LANES), lambda b, i: (i, 0))
    out_shape = (
        jax.ShapeDtypeStruct((nb, n, QK_WIDTH), BF16),
        jax.ShapeDtypeStruct((nb, n, KV_LORA), F32),
        jax.ShapeDtypeStruct((nb, n, LANES), F32),
        jax.ShapeDtypeStruct((nb, n, CONV_CH), BF16),
        jax.ShapeDtypeStruct((nb, n, GDN_WIDTH), BF16),
        jax.ShapeDtypeStruct((nb, n, LANES), F32),
        jax.ShapeDtypeStruct((nb, SUBLANES, CONV_CH), F32),
    )
    return pl.pallas_call(
        _in_proj_kernel,
        grid=grid,
        in_specs=[row(D_MODEL), _const_spec((1, D_MODEL)), _const_spec((D_MODEL, IN_COLS_PAD)),
                  _const_spec((1, Q_LORA)), _const_spec((Q_LORA, QK_WIDTH)), _const_spec((1, KV_LORA)),
                  tab, tab],
        out_specs=(row(QK_WIDTH), row(KV_LORA), row(LANES), row(CONV_CH), row(GDN_WIDTH), row(LANES),
                   pl.BlockSpec((None, SUBLANES, CONV_CH), lambda b, i: (b, 0, 0))),
        out_shape=out_shape,
        compiler_params=_params("parallel", "arbitrary"),
        name="in_proj",
    )(x, wp["g_attn"], wp["w_in"], wp["g_qa"], wp["w_uq"], wp["g_kv"], cq, sq)


def _kv_expand_kernel(lat_ref, krp_ref, w_uk_ref, w_uv_ref, ck_ref, sk_ref, k_ref, v_ref):
    latb = lat_ref[...].astype(BF16)
    k_raw = jnp.dot(latb, w_uk_ref[...], preferred_element_type=F32)
    _head_norm_rope(k_raw, krp_ref[...], ck_ref[...], sk_ref[...], k_ref)
    v_ref[...] = jnp.dot(latb, w_uv_ref[...], preferred_element_type=F32).astype(BF16)


def _kv_expand(lat, krp, ck, sk, wp):
    nb, n, _ = lat.shape
    tm = _row_tile(n, 512)
    row = lambda w: pl.BlockSpec((None, tm, w), lambda b, i: (b, i, 0))
    tab = pl.BlockSpec((tm, LANES), lambda b, i: (i, 0))
    return pl.pallas_call(
        _kv_expand_kernel,
        grid=(nb, n // tm),
        in_specs=[row(KV_LORA), row(LANES), _const_spec((KV_LORA, QK_WIDTH)),
                  _const_spec((KV_LORA, MLA_WIDTH)), tab, tab],
        out_specs=(row(QK_WIDTH), row(MLA_WIDTH)),
        out_shape=(jax.ShapeDtypeStruct((nb, n, QK_WIDTH), BF16),
                   jax.ShapeDtypeStruct((nb, n, MLA_WIDTH), BF16)),
        compiler_params=_params("parallel", "parallel"),
        name="kv_expand",
    )(lat, krp, wp["w_uk"], wp["w_uv"], ck, sk)


HEADS_PER_STEP = 2


def _attention_kernel(q_ref, k_ref, v_ref, ke_ref, ve_ref, o_ref, m_sc, l_sc, acc_sc, *, tq, tk, causal):
    i = pl.program_id(2)
    n_kv = k_ref.shape[0] // tk

    def q_of(hh):
        return q_ref[:, hh * LANES:(hh + 1) * LANES]

    def update(hh, s, v):
        m_old = m_sc[hh]
        m_new = jnp.maximum(m_old, jnp.max(s, axis=-1, keepdims=True))
        alpha = jnp.exp(m_old - m_new)
        p = jnp.exp(s - m_new)
        l_sc[hh] = alpha * l_sc[hh] + jnp.sum(p, axis=-1, keepdims=True)
        acc_sc[hh] = alpha * acc_sc[hh] + jnp.dot(p.astype(BF16), v, preferred_element_type=F32)
        m_sc[hh] = m_new

    for hh in range(HEADS_PER_STEP):
        s = _mm_nt(q_of(hh), ke_ref[:, hh * LANES:(hh + 1) * LANES])
        m = jnp.max(s, axis=-1, keepdims=True)
        p = jnp.exp(s - m)
        m_sc[hh] = m
        l_sc[hh] = jnp.sum(p, axis=-1, keepdims=True)
        acc_sc[hh] = jnp.dot(p.astype(BF16), ve_ref[:, hh * V_HEAD:(hh + 1) * V_HEAD],
                             preferred_element_type=F32)

    def full_tile(j, carry):
        start = pl.multiple_of(j * tk, tk)
        for hh in range(HEADS_PER_STEP):
            s = _mm_nt(q_of(hh), k_ref[pl.ds(start, tk), hh * LANES:(hh + 1) * LANES])
            update(hh, s, v_ref[pl.ds(start, tk), hh * V_HEAD:(hh + 1) * V_HEAD])
        return carry

    if causal:
        lax.fori_loop(0, i, full_tile, 0)
        start = pl.multiple_of(i * tk, tk)
        qc = lax.broadcasted_iota(jnp.int32, (tq, tk), 0) // CHUNK
        kc = lax.broadcasted_iota(jnp.int32, (tq, tk), 1) // CHUNK
        visible = kc <= qc
        for hh in range(HEADS_PER_STEP):
            s = _mm_nt(q_of(hh), k_ref[pl.ds(start, tk), hh * LANES:(hh + 1) * LANES])
            s = jnp.where(visible, s, -jnp.inf)
            update(hh, s, v_ref[pl.ds(start, tk), hh * V_HEAD:(hh + 1) * V_HEAD])
    else:
        lax.fori_loop(0, n_kv, full_tile, 0)

    for hh in range(HEADS_PER_STEP):
        o = acc_sc[hh] / l_sc[hh]
        o_ref[:, hh * V_HEAD:(hh + 1) * V_HEAD] = o.astype(o_ref.dtype)


def _attention(q, k, v, k_extra, v_extra, causal):
    nb, nq, _ = q.shape
    nk = k.shape[1]
    ne = k_extra.shape[1]
    shared_extra = k_extra.shape[0] == 1
    tk = _row_tile(nk, 256)
    tq = tk if causal else _row_tile(nq, 256)
    pairs = MLA_HEADS // HEADS_PER_STEP
    qk_w = HEADS_PER_STEP * LANES
    v_w = HEADS_PER_STEP * V_HEAD
    eb = (lambda b: 0) if shared_extra else (lambda b: b)
    kernel = functools.partial(_attention_kernel, tq=tq, tk=tk, causal=causal)
    return pl.pallas_call(
        kernel,
        grid=(nb, pairs, nq // tq),
        in_specs=[pl.BlockSpec((None, tq, qk_w), lambda b, p, i: (b, i, p)),
                  pl.BlockSpec((None, nk, qk_w), lambda b, p, i: (b, 0, p)),
                  pl.BlockSpec((None, nk, v_w), lambda b, p, i: (b, 0, p)),
                  pl.BlockSpec((None, ne, qk_w), lambda b, p, i: (eb(b), 0, p)),
                  pl.BlockSpec((None, ne, v_w), lambda b, p, i: (eb(b), 0, p))],
        out_specs=pl.BlockSpec((None, tq, v_w), lambda b, p, i: (b, i, p)),
        out_shape=jax.ShapeDtypeStruct((nb, nq, MLA_WIDTH), BF16),
        scratch_shapes=[pltpu.VMEM((HEADS_PER_STEP, tq, 1), F32),
                        pltpu.VMEM((HEADS_PER_STEP, tq, 1), F32),
                        pltpu.VMEM((HEADS_PER_STEP, tq, V_HEAD), F32)],
        compiler_params=_params("parallel", "parallel", "arbitrary"),
        name="attention",
    )(q, k, v, k_extra, v_extra)


def _gdn_prep_kernel(u_ref, uprev_ref, init_ref, w_ref, logit_ref, const_ref,
                     q_ref, k_ref, v_ref, gb_ref, gbt_ref, ext_sc, *, chunk):
    tm = u_ref.shape[0]
    i = pl.program_id(1)
    prev = jnp.where(i == 0, init_ref[...], uprev_ref[...].astype(F32))
    ext_sc[0:SUBLANES, :] = prev
    ext_sc[SUBLANES:SUBLANES + tm, :] = u_ref[...].astype(F32)
    y = w_ref[CONV_W - 1:CONV_W, :] * ext_sc[SUBLANES:SUBLANES + tm, :]
    for j in range(CONV_W - 1):
        off = SUBLANES - (CONV_W - 1) + j
        y = y + w_ref[j:j + 1, :] * ext_sc[off:off + tm, :]
    y = y * _sigmoid(y)
    for h in range(GDN_HEADS):
        lo = h * GDN_DK
        qh = y[:, lo:lo + GDN_DK]
        kh = y[:, GDN_WIDTH + lo:GDN_WIDTH + lo + GDN_DK]
        q_ref[:, lo:lo + GDN_DK] = (
            qh * (lax.rsqrt(jnp.sum(qh * qh, axis=-1, keepdims=True) + EPS) * GDN_DK ** -0.5)).astype(BF16)
        k_ref[:, lo:lo + GDN_DK] = (
            kh * lax.rsqrt(jnp.sum(kh * kh, axis=-1, keepdims=True) + EPS)).astype(BF16)
    v_ref[...] = y[:, 2 * GDN_WIDTH:].astype(BF16)

    lg = logit_ref[...]
    beta = _sigmoid(lg)
    z = lg + const_ref[1:2, :]
    softplus = jnp.maximum(z, 0.0) + jnp.log(1.0 + jnp.exp(-jnp.abs(z)))
    g = const_ref[0:1, :] * softplus
    row_in_chunk = lax.broadcasted_iota(jnp.int32, g.shape, 0) % chunk
    step = 1
    while step < chunk:
        g = g + jnp.where(row_in_chunk >= step, pltpu.roll(g, step, 0), 0.0)
        step *= 2
    lane = lax.broadcasted_iota(jnp.int32, g.shape, 1)
    gb = jnp.where(lane < GDN_HEADS, beta, g)
    gb_ref[...] = gb
    sel = (lax.broadcasted_iota(jnp.int32, (SUBLANES, LANES), 0)
           == lax.broadcasted_iota(jnp.int32, (SUBLANES, LANES), 1)).astype(F32)
    gbt_ref[...] = lax.dot_general(sel, gb, (((1,), (1,)), ((), ())), precision=lax.Precision.HIGHEST,
                                   preferred_element_type=F32)


def _gdn_prep(u, logits, init, wp, chunk):
    nb, n, _ = u.shape
    tm = _row_tile(n, 512)
    per = tm // SUBLANES
    shared_init = init.shape[0] == 1
    ib = (lambda b: 0) if shared_init else (lambda b: b)
    row = lambda w: pl.BlockSpec((None, tm, w), lambda b, i: (b, i, 0))
    kernel = functools.partial(_gdn_prep_kernel, chunk=chunk)
    return pl.pallas_call(
        kernel,
        grid=(nb, n // tm),
        in_specs=[row(CONV_CH),
                  pl.BlockSpec((None, SUBLANES, CONV_CH), lambda b, i: (b, jnp.maximum(i * per - 1, 0), 0)),
                  pl.BlockSpec((None, SUBLANES, CONV_CH), lambda b, i: (ib(b), 0, 0)),
                  _const_spec((CONV_W, CONV_CH)), row(LANES), _const_spec((SUBLANES, LANES))],
        out_specs=(row(GDN_WIDTH), row(GDN_WIDTH), row(GDN_WIDTH), row(LANES),
                   pl.BlockSpec((None, SUBLANES, tm), lambda b, i: (b, 0, i))),
        out_shape=(jax.ShapeDtypeStruct((nb, n, GDN_WIDTH), BF16),
                   jax.ShapeDtypeStruct((nb, n, GDN_WIDTH), BF16),
                   jax.ShapeDtypeStruct((nb, n, GDN_WIDTH), BF16),
                   jax.ShapeDtypeStruct((nb, n, LANES), F32),
                   jax.ShapeDtypeStruct((nb, SUBLANES, n), F32)),
        scratch_shapes=[pltpu.VMEM((tm + SUBLANES, CONV_CH), F32)],
        compiler_params=_params("parallel", "parallel"),
        name="gdn_prep",
    )(u, u, init, wp["conv_w"], logits, wp["gdn_const"])


def _unit_lower_inverse(a):
    c = a.shape[0]
    eye = (lax.broadcasted_iota(jnp.int32, (c, c), 0) == lax.broadcasted_iota(jnp.int32, (c, c), 1)).astype(F32)
    p = eye - a
    power = _mm(a, a)
    span = 2
    while span < c:
        if 2 * span >= c:
            p = p + _mm(p, power)
        else:
            both = _mm(jnp.concatenate([p, power], axis=0), power)
            p = p + both[:c]
            power = both[c:]
        span *= 2
    return p


def _gdn_scan_kernel(q_ref, k_ref, v_ref, gb_ref, gbt_ref, gate_ref, s0_ref, gnorm_ref,
                     o_ref, s_out_ref, s_sc, *, chunk):
    i = pl.program_id(1)
    tm = q_ref.shape[0]

    @pl.when(i == 0)
    def _():
        s_sc[...] = s0_ref[...]

    r = lax.broadcasted_iota(jnp.int32, (chunk, chunk), 0)
    c = lax.broadcasted_iota(jnp.int32, (chunk, chunk), 1)
    causal = r >= c
    strict = r > c
    gnorm = gnorm_ref[...]

    for ci in range(tm // chunk):
        rows = slice(ci * chunk, (ci + 1) * chunk)
        for h in range(GDN_HEADS):
            cols = slice(h * GDN_DK, (h + 1) * GDN_DK)
            qh = q_ref[rows, cols]
            kh = k_ref[rows, cols]
            kf = kh.astype(F32)
            vf = v_ref[rows, cols].astype(F32)
            beta = gb_ref[rows, h:h + 1]
            gc = gb_ref[rows, GDN_HEADS + h:GDN_HEADS + h + 1]
            gc_row = gbt_ref[GDN_HEADS + h:GDN_HEADS + h + 1, rows]
            g_last = gc[chunk - 1:chunk, :]
            decay = jnp.exp(jnp.where(causal, gc - gc_row, -jnp.inf))
            kb = kf * beta
            kq = _mm_nt(jnp.concatenate([kb.astype(BF16), qh], axis=0), kh)
            a = jnp.where(strict, kq[:chunk] * decay, 0.0)
            qk = kq[chunk:] * decay
            t_inv = _unit_lower_inverse(a)
            e_gc = jnp.exp(gc)
            uw = _mm(t_inv, jnp.concatenate([vf * beta, kb * e_gc], axis=1))
            s_old = s_sc[h]
            ws = _mm(jnp.concatenate([uw[:, GDN_DV:], qh.astype(F32) * e_gc], axis=0), s_old)
            v_new = uw[:, :GDN_DV] - ws[:chunk]
            o = ws[chunk:] + _mm(qk, v_new)
            s_sc[h] = s_old * jnp.exp(g_last) + _mm_tn(kf * jnp.exp(g_last - gc), v_new)
            gate = gate_ref[rows, cols].astype(F32)
            o = _rms(o, gnorm) * (gate * _sigmoid(gate))
            o_ref[rows, cols] = o.astype(o_ref.dtype)

    @pl.when(i == pl.num_programs(1) - 1)
    def _():
        s_out_ref[...] = s_sc[...]


def _gdn_scan(q, k, v, gb, gbt, gate, s0, wp, chunk):
    nb, n, _ = q.shape
    tm = _row_tile(n, 256)
    shared_s0 = s0.shape[0] == 1
    sb = (lambda b: 0) if shared_s0 else (lambda b: b)
    row = lambda w: pl.BlockSpec((None, tm, w), lambda b, i: (b, i, 0))
    state = (None, GDN_HEADS, GDN_DK, GDN_DV)
    kernel = functools.partial(_gdn_scan_kernel, chunk=chunk)
    return pl.pallas_call(
        kernel,
        grid=(nb, n // tm),
        in_specs=[row(GDN_WIDTH), row(GDN_WIDTH), row(GDN_WIDTH), row(LANES),
                  pl.BlockSpec((None, SUBLANES, tm), lambda b, i: (b, 0, i)),
                  row(GDN_WIDTH),
                  pl.BlockSpec(state, lambda b, i: (sb(b), 0, 0, 0)),
                  _const_spec((1, GDN_DV))],
        out_specs=(row(GDN_WIDTH), pl.BlockSpec(state, lambda b, i: (b, 0, 0, 0))),
        out_shape=(jax.ShapeDtypeStruct((nb, n, GDN_WIDTH), BF16),
                   jax.ShapeDtypeStruct((nb, GDN_HEADS, GDN_DK, GDN_DV), F32)),
        scratch_shapes=[pltpu.VMEM((GDN_HEADS, GDN_DK, GDN_DV), F32)],
        compiler_params=_params("parallel", "arbitrary"),
        name="gdn_scan",
    )(q, k, v, gb, gbt, gate, s0, wp["g_gdn"])


FF_SLAB = 1024


def _finish_kernel(x_ref, att_ref, gdn_ref, g_mla_ref, w_out_ref, g_mlp_ref, w_up_ref, w_down_ref, y_ref):
    att = _rms(att_ref[...].astype(F32), g_mla_ref[...])
    x1 = (x_ref[...] + _mm(att, w_out_ref[0:MLA_WIDTH, :])
          + jnp.dot(gdn_ref[...], w_out_ref[MLA_WIDTH:, :], preferred_element_type=F32))
    hb = _rms(x1, g_mlp_ref[...]).astype(BF16)
    y_ref[...] = x1
    for s in range(D_FF // FF_SLAB):
        up = jnp.dot(hb, w_up_ref[:, s * FF_SLAB:(s + 1) * FF_SLAB], preferred_element_type=F32)
        up = jnp.square(jnp.maximum(up, 0.0)).astype(BF16)
        y_ref[...] += jnp.dot(up, w_down_ref[s * FF_SLAB:(s + 1) * FF_SLAB, :], preferred_element_type=F32)


def _finish(x, att, gdn, wp):
    nb, n, _ = x.shape
    tm = _row_tile(n, 512)
    row = lambda w: pl.BlockSpec((None, tm, w), lambda b, i: (b, i, 0))
    return pl.pallas_call(
        _finish_kernel,
        grid=(nb, n // tm),
        in_specs=[row(D_MODEL), row(MLA_WIDTH), row(GDN_WIDTH), _const_spec((1, MLA_WIDTH)),
                  _const_spec((D_MODEL, D_MODEL)), _const_spec((1, D_MODEL)),
                  _const_spec((D_MODEL, D_FF)), _const_spec((D_FF, D_MODEL))],
        out_specs=row(D_MODEL),
        out_shape=jax.ShapeDtypeStruct((nb, n, D_MODEL), F32),
        compiler_params=_params("parallel", "parallel"),
        name="finish",
    )(x, att, gdn, wp["g_mla"], wp["w_out"], wp["g_mlp"], wp["w_up"], wp["w_down"])


def _pad_heads(w, head_dim, lane_mask):
    r = w.shape[0]
    w3 = w.reshape(r, MLA_HEADS, head_dim)
    idx = jnp.minimum(_SRC_IDX, head_dim - 1)
    return (jnp.take(w3, idx, axis=2) * lane_mask).reshape(r, MLA_HEADS * LANES)


def _prepare_params(attn_norm, w_in, q_a_norm, w_uq, kv_a_norm, w_uk, w_uv, q_norm, k_norm, mla_out_norm,
                    conv_w, a_log, dt_bias, gdn_out_norm, w_out, mlp_norm, w_up, w_down):
    split = [Q_LORA, Q_LORA + KV_LORA, Q_LORA + KV_LORA + ROPE_DIM]
    split.append(split[-1] + CONV_CH)
    split.append(split[-1] + GDN_WIDTH)
    w_q, w_kv, w_kr, w_u, w_gate, w_logit = jnp.split(w_in, split, axis=1)
    zeros = lambda c: jnp.zeros((D_MODEL, c), w_in.dtype)
    w_kr_placed = jnp.concatenate([w_kr[:, :HALF_ROPE], zeros(64 - HALF_ROPE), w_kr[:, HALF_ROPE:],
                                   zeros(64 - HALF_ROPE)], axis=1)
    w_in_p = jnp.concatenate([w_q, w_kv, w_kr_placed, w_u, w_gate, w_logit,
                              zeros(LANES - 2 * GDN_HEADS)], axis=1)
    lane_pad = (0, LANES - 2 * GDN_HEADS)
    neg_a = jnp.pad(jnp.concatenate([jnp.zeros((GDN_HEADS,), F32), -jnp.exp(a_log.astype(F32))]), lane_pad)
    dtb = jnp.pad(jnp.concatenate([jnp.zeros((GDN_HEADS,), F32), dt_bias.astype(F32)]), lane_pad)
    gdn_const = jnp.zeros((SUBLANES, LANES), F32).at[0].set(neg_a).at[1].set(dtb)
    return {
        "g_attn": attn_norm.reshape(1, D_MODEL).astype(F32),
        "w_in": w_in_p.astype(BF16),
        "g_qa": q_a_norm.reshape(1, Q_LORA).astype(F32),
        "w_uq": _pad_heads(w_uq, QK_HEAD, _SRC_ANY).astype(BF16),
        "g_kv": kv_a_norm.reshape(1, KV_LORA).astype(F32),
        "w_uk": _pad_heads(w_uk, NOPE_DIM, _SRC_NOPE).astype(BF16),
        "w_uv": w_uv.astype(BF16),
        "g_q": jnp.take(q_norm.astype(F32), _SRC_IDX) * _SRC_ANY,
        "g_k": jnp.take(k_norm.astype(F32), _SRC_IDX) * _SRC_ANY,
        "g_mla": mla_out_norm.reshape(1, MLA_WIDTH).astype(F32),
        "conv_w": conv_w.astype(F32),
        "gdn_const": gdn_const,
        "g_gdn": gdn_out_norm.reshape(1, GDN_DV).astype(F32),
        "w_out": w_out.astype(BF16),
        "g_mlp": mlp_norm.reshape(1, D_MODEL).astype(F32),
        "w_up": w_up.astype(BF16),
        "w_down": w_down.astype(BF16),
    }


def _rope_tables(pos, gain, scale):
    inv_freq = ROPE_BASE ** (-jnp.arange(HALF_ROPE, dtype=F32) / HALF_ROPE)
    ang = pos.astype(F32)[:, None] * inv_freq[None, :]
    cos, sin = jnp.cos(ang), jnp.sin(ang)
    n = pos.shape[0]
    ones = lambda c: jnp.ones((n, c), F32)
    zeros = lambda c: jnp.zeros((n, c), F32)
    c_tab = jnp.concatenate([cos, ones(64 - HALF_ROPE), cos, ones(64 - HALF_ROPE)], axis=1)
    s_tab = jnp.concatenate([-sin, zeros(64 - HALF_ROPE), sin, zeros(64 - HALF_ROPE)], axis=1)
    return c_tab * (gain * scale)[None, :], s_tab * (jnp.roll(gain, 64) * scale)[None, :]


def _place_rope_key(kr):
    z = jnp.zeros(kr.shape[:-1] + (64 - HALF_ROPE,), kr.dtype)
    return jnp.concatenate([kr[..., :HALF_ROPE], z, kr[..., HALF_ROPE:], z], axis=-1)


def _unplace_rope_key(krp):
    return jnp.concatenate([krp[..., :HALF_ROPE], krp[..., 64:64 + HALF_ROPE]], axis=-1)


def kernel(x_prompt, x_sample, cache_kv_latent, cache_k_rope, state_gdn, state_conv, meta_tokens, attn_norm,
           w_in, q_a_norm, w_uq, kv_a_norm, w_uk, w_uv, q_norm, k_norm, mla_out_norm, conv_w, a_log, dt_bias,
           gdn_out_norm, w_out, mlp_norm, w_up, w_down):
    assert attn_norm.shape[0] == 1, "single layer"
    wp = _prepare_params(attn_norm[0], w_in[0], q_a_norm[0], w_uq[0], kv_a_norm[0], w_uk[0], w_uv[0],
                         q_norm[0], k_norm[0], mla_out_norm[0], conv_w[0], a_log[0], dt_bias[0],
                         gdn_out_norm[0], w_out[0], mlp_norm[0], w_up[0], w_down[0])
    nb, seq, _ = x_prompt.shape
    db, dseq, _ = x_sample.shape
    n_cache = cache_kv_latent.shape[2]
    past = n_cache - N_META
    q_scale = QK_HEAD ** -0.5

    pos_meta = jnp.arange(N_META)
    pos_frames = N_META + jnp.arange(seq)
    pos_cache = N_META + jnp.arange(past)
    pos_new = n_cache + jnp.arange(dseq)
    tabs_q = lambda pos: _rope_tables(pos, wp["g_q"], q_scale)
    tabs_k = lambda pos: _rope_tables(pos, wp["g_k"], 1.0)

    x_meta = meta_tokens.astype(F32)[None]
    _, m_lat, m_krp, m_u, _, m_logit, m_tail = _in_proj(x_meta, *tabs_q(pos_meta), wp)
    m_k, m_v = _kv_expand(m_lat, m_krp, *tabs_k(pos_meta), wp)
    zero_tail = jnp.zeros((1, SUBLANES, CONV_CH), F32)
    m_q, m_kk, m_vv, m_gb, m_gbt = _gdn_prep(m_u, m_logit, zero_tail, wp, N_META)
    zero_state = jnp.zeros((1, GDN_HEADS, GDN_DK, GDN_DV), F32)
    m_gate = jnp.zeros((1, N_META, GDN_WIDTH), BF16)
    _, m_state = _gdn_scan(m_q, m_kk, m_vv, m_gb, m_gbt, m_gate, zero_state, wp, N_META)

    p_q, p_lat, p_krp, p_u, p_gate, p_logit, p_tail = _in_proj(x_prompt, *tabs_q(pos_frames), wp)
    p_k, p_v = _kv_expand(p_lat, p_krp, *tabs_k(pos_frames), wp)
    p_att = _attention(p_q, p_k, p_v, m_k, m_v, causal=True)
    g_q, g_k, g_v, g_gb, g_gbt = _gdn_prep(p_u, p_logit, m_tail, wp, CHUNK)
    p_gdn, p_state = _gdn_scan(g_q, g_k, g_v, g_gb, g_gbt, p_gate, m_state, wp, CHUNK)
    y_prompt = _finish(x_prompt, p_att, p_gdn, wp)

    s_q, s_lat, s_krp, s_u, s_gate, s_logit, s_tail = _in_proj(x_sample, *tabs_q(pos_new), wp)
    s_k, s_v = _kv_expand(s_lat, s_krp, *tabs_k(pos_new), wp)
    c_lat = cache_kv_latent[0].astype(F32)
    c_krp = _place_rope_key(cache_k_rope[0].astype(F32))
    cm_k, cm_v = _kv_expand(c_lat[:, :N_META], c_krp[:, :N_META], *tabs_k(pos_meta), wp)
    cf_k, cf_v = _kv_expand(c_lat[:, N_META:], c_krp[:, N_META:], *tabs_k(pos_cache), wp)
    e_k = jnp.concatenate([cm_k, s_k], axis=1)
    e_v = jnp.concatenate([cm_v, s_v], axis=1)
    s_att = _attention(s_q, cf_k, cf_v, e_k, e_v, causal=False)
    conv0 = jnp.pad(state_conv[0].astype(F32), ((0, 0), (SUBLANES - (CONV_W - 1), 0), (0, 0)))
    d_chunk = CHUNK if dseq % CHUNK == 0 else dseq
    d_q, d_k, d_v, d_gb, d_gbt = _gdn_prep(s_u, s_logit, conv0, wp, d_chunk)
    s_gdn, s_state = _gdn_scan(d_q, d_k, d_v, d_gb, d_gbt, s_gate, state_gdn[0].astype(F32), wp, d_chunk)
    y_sample = _finish(x_sample, s_att, s_gdn, wp)

    tail_rows = slice(SUBLANES - (CONV_W - 1), SUBLANES)
    bcast = lambda a: jnp.broadcast_to(a, (nb,) + a.shape[1:])
    p_kv_latent = jnp.concatenate([bcast(m_lat), p_lat], axis=1)[None]
    p_k_rope = _unplace_rope_key(jnp.concatenate([bcast(m_krp), p_krp], axis=1))[None]
    return (y_prompt, y_sample, p_kv_latent, p_k_rope, p_state[None], p_tail[:, tail_rows][None],
            s_lat[None], _unplace_rope_key(s_krp)[None], s_state[None], s_tail[:, tail_rows][None])
```

```python
import functools
import math

import jax
import jax.numpy as jnp
import numpy as np
from jax import lax
from jax.experimental import pallas as pl
from jax.experimental.pallas import tpu as pltpu

F32 = jnp.float32
BF16 = jnp.bfloat16

D_MODEL = 1024
N_META = 16
CHUNK = 64
EPS = 1e-6
MLA_HEADS = 8
Q_LORA = 384
KV_LORA = 256
NOPE_DIM = 64
ROPE_DIM = 32
QK_HEAD = NOPE_DIM + ROPE_DIM
V_HEAD = 64
MLA_WIDTH = MLA_HEADS * V_HEAD
ROPE_BASE = 10000.0
GDN_HEADS = 4
GDN_DK = 128
GDN_DV = 128
GDN_WIDTH = GDN_HEADS * GDN_DK
CONV_W = 4
CONV_CH = 3 * GDN_WIDTH
D_FF = 4 * D_MODEL

LANES = 128
SUBLANES = 8
QK_WIDTH = MLA_HEADS * LANES
HALF_ROPE = ROPE_DIM // 2
VMEM_LIMIT = 56 * 1024 * 1024
LOG2_E = math.log2(math.e)

COL_Q = 0
COL_KV = COL_Q + Q_LORA
COL_KR = COL_KV + KV_LORA
COL_U = COL_KR + LANES
COL_GATE = COL_U + CONV_CH
COL_LOGIT = COL_GATE + GDN_WIDTH
IN_COLS_PAD = COL_LOGIT + LANES


def _head_lane_source():
    src = [-1] * LANES
    for d in range(HALF_ROPE):
        src[d] = NOPE_DIM + d
        src[64 + d] = NOPE_DIM + HALF_ROPE + d
    for d in range(48):
        src[16 + d] = d
    for d in range(16):
        src[80 + d] = 48 + d
    return src


_SRC = _head_lane_source()
_SRC_IDX = np.array([max(s, 0) for s in _SRC], np.int32)
_SRC_ANY = np.array([1.0 if s >= 0 else 0.0 for s in _SRC], np.float32)
_SRC_NOPE = np.array([1.0 if 0 <= s < NOPE_DIM else 0.0 for s in _SRC], np.float32)


def _mm(a, b):
    return jnp.dot(a.astype(BF16), b.astype(BF16), preferred_element_type=F32)


def _mm_nt(a, b):
    return lax.dot_general(a.astype(BF16), b.astype(BF16), (((1,), (1,)), ((), ())),
                           preferred_element_type=F32)


def _mm_tn(a, b):
    return lax.dot_general(a.astype(BF16), b.astype(BF16), (((0,), (0,)), ((), ())),
                           preferred_element_type=F32)


def _rms(x, g):
    return x * lax.rsqrt(jnp.mean(x * x, axis=-1, keepdims=True) + EPS) * g


def _sigmoid(x):
    return 1.0 / (1.0 + jnp.exp(-x))


def _row_tile(n, cap):
    t = cap
    while t > 1 and n % t:
        t //= 2
    return t


def _params(*sem):
    return pltpu.CompilerParams(dimension_semantics=sem, vmem_limit_bytes=VMEM_LIMIT)


def _const_spec(shape):
    nd = len(shape)
    return pl.BlockSpec(shape, lambda *_: (0,) * nd, pipeline_mode=pl.Buffered(1))


def _head_norm_rope(raw, extra, c_tab, s_tab, out_ref):
    for h in range(MLA_HEADS):
        seg = raw[:, h * LANES:(h + 1) * LANES]
        if extra is not None:
            seg = seg + extra
        r = lax.rsqrt(jnp.sum(seg * seg, axis=-1, keepdims=True) * (1.0 / QK_HEAD) + EPS)
        out = (seg * c_tab + pltpu.roll(seg, 64, 1) * s_tab) * r
        out_ref[:, h * LANES:(h + 1) * LANES] = out.astype(out_ref.dtype)


def _in_proj_kernel(x_ref, g_attn_ref, w_in_ref, g_qa_ref, w_uq_ref, g_kv_ref, cq_ref, sq_ref,
                    q_ref, lat_ref, krp_ref, u_ref, gate_ref, logit_ref, tail_ref):
    x = x_ref[...]
    hb = _rms(x, g_attn_ref[...]).astype(BF16)

    def proj(lo, hi):
        return jnp.dot(hb, w_in_ref[:, lo:hi], preferred_element_type=F32)

    q_lat = _rms(proj(COL_Q, COL_KV), g_qa_ref[...])
    q_raw = _mm(q_lat, w_uq_ref[...])
    _head_norm_rope(q_raw, None, cq_ref[...], sq_ref[...], q_ref)
    lat_ref[...] = _rms(proj(COL_KV, COL_KR), g_kv_ref[...])
    krp_ref[...] = proj(COL_KR, COL_U)
    u = proj(COL_U, COL_GATE)
    u_ref[...] = u.astype(BF16)
    tail_ref[...] = u[u.shape[0] - SUBLANES:, :]
    gate_ref[...] = proj(COL_GATE, COL_LOGIT).astype(BF16)
    logit_ref[...] = proj(COL_LOGIT, IN_COLS_PAD)


def _in_proj(x, cq, sq, wp):
    nb, n, _ = x.shape
    tm = _row_tile(n, 512)
    grid = (nb, n // tm)
    row = lambda w: pl.BlockSpec((None, tm, w), lambda b, i: (b, i, 0))
    tab = pl.BlockSpec((tm, LANES), lambda b, i: (i, 0))
    out_shape = (
        jax.ShapeDtypeStruct((nb, n, QK_WIDTH), BF16),
        jax.ShapeDtypeStruct((nb, n, KV_LORA), F32),
        jax.ShapeDtypeStruct((nb, n, LANES), F32),
        jax.ShapeDtypeStruct((nb, n, CONV_CH), BF16),
        jax.ShapeDtypeStruct((nb, n, GDN_WIDTH), BF16),
        jax.ShapeDtypeStruct((nb, n, LANES), F32),
        jax.ShapeDtypeStruct((nb, SUBLANES, CONV_CH), F32),
    )
    return pl.pallas_call(
        _in_proj_kernel,
        grid=grid,
        in_specs=[row(D_MODEL), _const_spec((1, D_MODEL)), _const_spec((D_MODEL, IN_COLS_PAD)),
                  _const_spec((1, Q_LORA)), _const_spec((Q_LORA, QK_WIDTH)), _const_spec((1, KV_LORA)),
                  tab, tab],
        out_specs=(row(QK_WIDTH), row(KV_LORA), row(LANES), row(CONV_CH), row(GDN_WIDTH), row(LANES),
                   pl.BlockSpec((None, SUBLANES, CONV_CH), lambda b, i: (b, 0, 0))),
        out_shape=out_shape,
        compiler_params=_params("parallel", "arbitrary"),
        name="in_proj",
    )(x, wp["g_attn"], wp["w_in"], wp["g_qa"], wp["w_uq"], wp["g_kv"], cq, sq)


def _kv_expand_kernel(lat_ref, krp_ref, w_uk_ref, w_uv_ref, ck_ref, sk_ref, k_ref, v_ref):
    latb = lat_ref[...].astype(BF16)
    k_raw = jnp.dot(latb, w_uk_ref[...], preferred_element_type=F32)
    _head_norm_rope(k_raw, krp_ref[...], ck_ref[...], sk_ref[...], k_ref)
    v_ref[...] = jnp.dot(latb, w_uv_ref[...], preferred_element_type=F32).astype(BF16)


def _kv_expand(lat, krp, ck, sk, wp):
    nb, n, _ = lat.shape
    tm = _row_tile(n, 512)
    row = lambda w: pl.BlockSpec((None, tm, w), lambda b, i: (b, i, 0))
    tab = pl.BlockSpec((tm, LANES), lambda b, i: (i, 0))
    return pl.pallas_call(
        _kv_expand_kernel,
        grid=(nb, n // tm),
        in_specs=[row(KV_LORA), row(LANES), _const_spec((KV_LORA, QK_WIDTH)),
                  _const_spec((KV_LORA, MLA_WIDTH)), tab, tab],
        out_specs=(row(QK_WIDTH), row(MLA_WIDTH)),
        out_shape=(jax.ShapeDtypeStruct((nb, n, QK_WIDTH), BF16),
                   jax.ShapeDtypeStruct((nb, n, MLA_WIDTH), BF16)),
        compiler_params=_params("parallel", "parallel"),
        name="kv_expand",
    )(lat, krp, wp["w_uk"], wp["w_uv"], ck, sk)


def _lane_tile(m, n):
    if n % LANES:
        return m[:, :n]
    return m if n == LANES else jnp.concatenate([m] * (n // LANES), axis=1)


def _online_softmax_step(s_pieces, v_pieces, m_old, acc_old):
    heads = range(len(s_pieces))
    row_max = []
    for h in heads:
        r = jnp.max(s_pieces[h][0], axis=-1, keepdims=True)
        for piece in s_pieces[h][1:]:
            r = jnp.maximum(r, jnp.max(piece, axis=-1, keepdims=True))
        row_max.append(r)
    if m_old is None:
        m_new = [jnp.broadcast_to(row_max[h], (row_max[h].shape[0], LANES)) for h in heads]
    else:
        m_new = [jnp.maximum(m_old[h], row_max[h]) for h in heads]
    probs = [[jnp.exp2(piece - _lane_tile(m_new[h], piece.shape[1])).astype(BF16) for piece in s_pieces[h]]
             for h in heads]
    pv = []
    for h in heads:
        t = jnp.dot(probs[h][0], v_pieces[h][0], preferred_element_type=F32)
        for p, v in zip(probs[h][1:], v_pieces[h][1:]):
            t = t + jnp.dot(p, v, preferred_element_type=F32)
        pv.append(t)
    if acc_old is None:
        return m_new, pv
    return m_new, [jnp.exp2(m_old[h] - m_new[h]) * acc_old[h] + pv[h] for h in heads]


def _attention_kernel(q_ref, k_ref, v_ref, ke_ref, ve_ref, o_ref, m_sc, acc_sc, *, tq, tk, causal):
    i = pl.program_id(1)
    n_kv = k_ref.shape[0] // tk
    heads = range(MLA_HEADS)
    low_half = lax.broadcasted_iota(jnp.int32, (1, LANES), 1) < V_HEAD
    one = jnp.ones((), BF16)

    def q_of(h):
        return q_ref[:, h * LANES:(h + 1) * LANES]

    def values(ref, rows):
        out = []
        for pair in range(MLA_HEADS // 2):
            vp = ref[rows, pair * LANES:(pair + 1) * LANES]
            out += [jnp.where(low_half, vp, one), jnp.where(low_half, one, vp)]
        return out

    def scores(ref, rows):
        return [_mm_nt(q_of(h), ref[rows, h * LANES:(h + 1) * LANES]) for h in heads]

    s_first = [[s] for s in scores(ke_ref, slice(None))]
    v_first = [[v] for v in values(ve_ref, slice(None))]
    if causal:
        diag = pl.ds(pl.multiple_of(i * tk, tk), tk)
        qc = lax.broadcasted_iota(jnp.int32, (tq, tk), 0) // CHUNK
        kc = lax.broadcasted_iota(jnp.int32, (tq, tk), 1) // CHUNK
        visible = kc <= qc
        s_diag = scores(k_ref, diag)
        v_diag = values(v_ref, diag)
        for h in heads:
            s_first[h].append(jnp.where(visible, s_diag[h], -jnp.inf))
            v_first[h].append(v_diag[h])
    m0, acc0 = _online_softmax_step(s_first, v_first, None, None)
    for h in heads:
        m_sc[h] = m0[h]
        acc_sc[h] = acc0[h]

    def full_tiles(first, count):
        tiles = [pl.ds(pl.multiple_of((first + t) * tk, tk), tk) for t in range(count)]
        m_old = [m_sc[h] for h in heads]
        acc_old = [acc_sc[h] for h in heads]
        s_tiles = [scores(k_ref, rows) for rows in tiles]
        v_tiles = [values(v_ref, rows) for rows in tiles]
        s = [[s_tiles[t][h] for t in range(count)] for h in heads]
        v = [[v_tiles[t][h] for t in range(count)] for h in heads]
        m_new, acc_new = _online_softmax_step(s, v, m_old, acc_old)
        for h in heads:
            m_sc[h] = m_new[h]
            acc_sc[h] = acc_new[h]

    def tile_pair(j, carry):
        full_tiles(2 * j, 2)
        return carry

    if causal:
        lax.fori_loop(0, i // 2, tile_pair, 0)

        @pl.when(i % 2 == 1)
        def _():
            full_tiles(i - 1, 1)
    else:
        lax.fori_loop(0, n_kv // 2, tile_pair, 0)
        if n_kv % 2:
            full_tiles(n_kv - 1, 1)

    for pair in range(MLA_HEADS // 2):
        even = acc_sc[2 * pair]
        odd = acc_sc[2 * pair + 1]
        o = jnp.where(low_half, even / pltpu.roll(even, V_HEAD, 1), odd / pltpu.roll(odd, V_HEAD, 1))
        o_ref[:, pair * LANES:(pair + 1) * LANES] = o.astype(o_ref.dtype)


def _attention(q, k, v, k_extra, v_extra, causal):
    nb, nq, _ = q.shape
    nk = k.shape[1]
    ne = k_extra.shape[1]
    shared_extra = k_extra.shape[0] == 1
    tk = _row_tile(nk, 256)
    tq = tk if causal else _row_tile(nq, 256)
    eb = (lambda b: 0) if shared_extra else (lambda b: b)
    kernel = functools.partial(_attention_kernel, tq=tq, tk=tk, causal=causal)
    return pl.pallas_call(
        kernel,
        grid=(nb, nq // tq),
        in_specs=[pl.BlockSpec((None, tq, QK_WIDTH), lambda b, i: (b, i, 0)),
                  pl.BlockSpec((None, nk, QK_WIDTH), lambda b, i: (b, 0, 0)),
                  pl.BlockSpec((None, nk, MLA_WIDTH), lambda b, i: (b, 0, 0)),
                  pl.BlockSpec((None, ne, QK_WIDTH), lambda b, i: (eb(b), 0, 0)),
                  pl.BlockSpec((None, ne, MLA_WIDTH), lambda b, i: (eb(b), 0, 0))],
        out_specs=pl.BlockSpec((None, tq, MLA_WIDTH), lambda b, i: (b, i, 0)),
        out_shape=jax.ShapeDtypeStruct((nb, nq, MLA_WIDTH), BF16),
        scratch_shapes=[pltpu.VMEM((MLA_HEADS, tq, LANES), F32),
                        pltpu.VMEM((MLA_HEADS, tq, LANES), F32)],
        compiler_params=_params("parallel", "arbitrary"),
        name="attention",
    )(q, k, v, k_extra, v_extra)


def _gdn_prep_kernel(u_ref, uprev_ref, init_ref, w_ref, logit_ref, const_ref,
                     q_ref, k_ref, v_ref, gb_ref, gbt_ref, ext_sc, *, chunk):
    tm = u_ref.shape[0]
    i = pl.program_id(1)
    prev = jnp.where(i == 0, init_ref[...], uprev_ref[...].astype(F32))
    ext_sc[0:SUBLANES, :] = prev
    ext_sc[SUBLANES:SUBLANES + tm, :] = u_ref[...].astype(F32)
    y = w_ref[CONV_W - 1:CONV_W, :] * ext_sc[SUBLANES:SUBLANES + tm, :]
    for j in range(CONV_W - 1):
        off = SUBLANES - (CONV_W - 1) + j
        y = y + w_ref[j:j + 1, :] * ext_sc[off:off + tm, :]
    y = y * _sigmoid(y)
    for h in range(GDN_HEADS):
        lo = h * GDN_DK
        qh = y[:, lo:lo + GDN_DK]
        kh = y[:, GDN_WIDTH + lo:GDN_WIDTH + lo + GDN_DK]
        q_ref[:, lo:lo + GDN_DK] = (
            qh * (lax.rsqrt(jnp.sum(qh * qh, axis=-1, keepdims=True) + EPS) * GDN_DK ** -0.5)).astype(BF16)
        k_ref[:, lo:lo + GDN_DK] = (
            kh * lax.rsqrt(jnp.sum(kh * kh, axis=-1, keepdims=True) + EPS)).astype(BF16)
    v_ref[...] = y[:, 2 * GDN_WIDTH:].astype(BF16)

    lg = logit_ref[...]
    beta = _sigmoid(lg)
    z = lg + const_ref[1:2, :]
    softplus = jnp.maximum(z, 0.0) + jnp.log(1.0 + jnp.exp(-jnp.abs(z)))
    g = const_ref[0:1, :] * softplus
    row_in_chunk = lax.broadcasted_iota(jnp.int32, g.shape, 0) % chunk
    step = 1
    while step < chunk:
        g = g + jnp.where(row_in_chunk >= step, pltpu.roll(g, step, 0), 0.0)
        step *= 2
    lane = lax.broadcasted_iota(jnp.int32, g.shape, 1)
    gb = jnp.where(lane < GDN_HEADS, beta, g)
    gb_ref[...] = gb
    sel = (lax.broadcasted_iota(jnp.int32, (SUBLANES, LANES), 0)
           == lax.broadcasted_iota(jnp.int32, (SUBLANES, LANES), 1)).astype(F32)
    gbt_ref[...] = lax.dot_general(sel, gb, (((1,), (1,)), ((), ())), precision=lax.Precision.HIGHEST,
                                   preferred_element_type=F32)


def _gdn_prep(u, logits, init, wp, chunk):
    nb, n, _ = u.shape
    tm = _row_tile(n, 512)
    per = tm // SUBLANES
    shared_init = init.shape[0] == 1
    ib = (lambda b: 0) if shared_init else (lambda b: b)
    row = lambda w: pl.BlockSpec((None, tm, w), lambda b, i: (b, i, 0))
    kernel = functools.partial(_gdn_prep_kernel, chunk=chunk)
    return pl.pallas_call(
        kernel,
        grid=(nb, n // tm),
        in_specs=[row(CONV_CH),
                  pl.BlockSpec((None, SUBLANES, CONV_CH), lambda b, i: (b, jnp.maximum(i * per - 1, 0), 0)),
                  pl.BlockSpec((None, SUBLANES, CONV_CH), lambda b, i: (ib(b), 0, 0)),
                  _const_spec((CONV_W, CONV_CH)), row(LANES), _const_spec((SUBLANES, LANES))],
        out_specs=(row(GDN_WIDTH), row(GDN_WIDTH), row(GDN_WIDTH), row(LANES),
                   pl.BlockSpec((None, SUBLANES, tm), lambda b, i: (b, 0, i))),
        out_shape=(jax.ShapeDtypeStruct((nb, n, GDN_WIDTH), BF16),
                   jax.ShapeDtypeStruct((nb, n, GDN_WIDTH), BF16),
                   jax.ShapeDtypeStruct((nb, n, GDN_WIDTH), BF16),
                   jax.ShapeDtypeStruct((nb, n, LANES), F32),
                   jax.ShapeDtypeStruct((nb, SUBLANES, n), F32)),
        scratch_shapes=[pltpu.VMEM((tm + SUBLANES, CONV_CH), F32)],
        compiler_params=_params("parallel", "parallel"),
        name="gdn_prep",
    )(u, u, init, wp["conv_w"], logits, wp["gdn_const"])


def _unit_lower_inverses(a_list):
    c = a_list[0].shape[0]
    items = range(len(a_list))
    eye = (lax.broadcasted_iota(jnp.int32, (c, c), 0) == lax.broadcasted_iota(jnp.int32, (c, c), 1)).astype(F32)
    p = [eye - a for a in a_list]
    power = [_mm(a, a) for a in a_list]
    span = 2
    while span < c:
        if 2 * span >= c:
            p = [p[n] + _mm(p[n], power[n]) for n in items]
        else:
            both = [_mm(jnp.concatenate([p[n], power[n]], axis=0), power[n]) for n in items]
            p = [p[n] + both[n][:c] for n in items]
            power = [both[n][c:] for n in items]
        span *= 2
    return p


def _gdn_scan_kernel(q_ref, k_ref, v_ref, gb_ref, gbt_ref, gate_ref, s0_ref, gnorm_ref,
                     o_ref, s_out_ref, s_sc, *, chunk):
    i = pl.program_id(1)
    tm = q_ref.shape[0]
    n_chunks = tm // chunk
    heads = range(GDN_HEADS)

    @pl.when(i == 0)
    def _():
        s_sc[...] = s0_ref[...]

    r = lax.broadcasted_iota(jnp.int32, (chunk, chunk), 0)
    c = lax.broadcasted_iota(jnp.int32, (chunk, chunk), 1)
    causal = r >= c
    strict = r > c

    items = [(ci, h) for ci in range(n_chunks) for h in heads]
    rows_of = lambda ci: slice(ci * chunk, (ci + 1) * chunk)
    cols_of = lambda h: slice(h * GDN_DK, (h + 1) * GDN_DK)
    qh = [q_ref[rows_of(ci), cols_of(h)] for ci, h in items]
    kh = [k_ref[rows_of(ci), cols_of(h)] for ci, h in items]
    kf = [k.astype(F32) for k in kh]
    vf = [v_ref[rows_of(ci), cols_of(h)].astype(F32) for ci, h in items]
    beta = [gb_ref[rows_of(ci), h:h + 1] for ci, h in items]
    gc = [gb_ref[rows_of(ci), GDN_HEADS + h:GDN_HEADS + h + 1] for ci, h in items]
    gc_row = [gbt_ref[GDN_HEADS + h:GDN_HEADS + h + 1, rows_of(ci)] for ci, h in items]
    n_items = range(len(items))
    g_last = [gc[n][chunk - 1:chunk, :] for n in n_items]
    decay = [jnp.exp(jnp.where(causal, gc[n] - gc_row[n], -jnp.inf)) for n in n_items]
    e_gc = [jnp.exp(gc[n]) for n in n_items]
    kb = [kf[n] * beta[n] for n in n_items]
    kq = [_mm_nt(jnp.concatenate([kb[n].astype(BF16), qh[n]], axis=0), kh[n]) for n in n_items]
    a = [jnp.where(strict, kq[n][:chunk] * decay[n], 0.0) for n in n_items]
    qk = [(kq[n][chunk:] * decay[n]).astype(BF16) for n in n_items]
    t_inv = _unit_lower_inverses(a)
    uw = [_mm(t_inv[n], jnp.concatenate([vf[n] * beta[n], kb[n] * e_gc[n]], axis=1)) for n in n_items]
    wq = [jnp.concatenate([uw[n][:, GDN_DV:], qh[n].astype(F32) * e_gc[n]], axis=0).astype(BF16)
          for n in n_items]
    k_tail = [(kf[n] * jnp.exp(g_last[n] - gc[n])).astype(BF16) for n in n_items]
    s_decay = [jnp.exp(g_last[n]) for n in n_items]

    gnorm = gnorm_ref[...]
    state = [s_sc[h] for h in heads]
    for ci in range(n_chunks):
        idx = [ci * GDN_HEADS + h for h in heads]
        ws = [_mm(wq[n], state[h]) for h, n in zip(heads, idx)]
        v_new = [(uw[n][:, :GDN_DV] - ws[h][:chunk]).astype(BF16) for h, n in zip(heads, idx)]
        o = [ws[h][chunk:] + jnp.dot(qk[n], v_new[h], preferred_element_type=F32) for h, n in zip(heads, idx)]
        state = [state[h] * s_decay[n] + _mm_tn(k_tail[n], v_new[h]) for h, n in zip(heads, idx)]
        for h in heads:
            gate = gate_ref[rows_of(ci), cols_of(h)].astype(F32)
            o_ref[rows_of(ci), cols_of(h)] = (_rms(o[h], gnorm) * (gate * _sigmoid(gate))).astype(o_ref.dtype)
    for h in heads:
        s_sc[h] = state[h]

    @pl.when(i == pl.num_programs(1) - 1)
    def _():
        s_out_ref[...] = s_sc[...]


def _gdn_scan(q, k, v, gb, gbt, gate, s0, wp, chunk):
    nb, n, _ = q.shape
    tm = _row_tile(n, 256)
    shared_s0 = s0.shape[0] == 1
    sb = (lambda b: 0) if shared_s0 else (lambda b: b)
    row = lambda w: pl.BlockSpec((None, tm, w), lambda b, i: (b, i, 0))
    state = (None, GDN_HEADS, GDN_DK, GDN_DV)
    kernel = functools.partial(_gdn_scan_kernel, chunk=chunk)
    return pl.pallas_call(
        kernel,
        grid=(nb, n // tm),
        in_specs=[row(GDN_WIDTH), row(GDN_WIDTH), row(GDN_WIDTH), row(LANES),
                  pl.BlockSpec((None, SUBLANES, tm), lambda b, i: (b, 0, i)),
                  row(GDN_WIDTH),
                  pl.BlockSpec(state, lambda b, i: (sb(b), 0, 0, 0)),
                  _const_spec((1, GDN_DV))],
        out_specs=(row(GDN_WIDTH), pl.BlockSpec(state, lambda b, i: (b, 0, 0, 0))),
        out_shape=(jax.ShapeDtypeStruct((nb, n, GDN_WIDTH), BF16),
                   jax.ShapeDtypeStruct((nb, GDN_HEADS, GDN_DK, GDN_DV), F32)),
        scratch_shapes=[pltpu.VMEM((GDN_HEADS, GDN_DK, GDN_DV), F32)],
        compiler_params=_params("parallel", "arbitrary"),
        name="gdn_scan",
    )(q, k, v, gb, gbt, gate, s0, wp["g_gdn"])


FF_SLAB = 1024


def _finish_kernel(x_ref, att_ref, gdn_ref, g_mla_ref, w_out_ref, g_mlp_ref, w_up_ref, w_down_ref, y_ref):
    att = _rms(att_ref[...].astype(F32), g_mla_ref[...])
    x1 = (x_ref[...] + _mm(att, w_out_ref[0:MLA_WIDTH, :])
          + jnp.dot(gdn_ref[...], w_out_ref[MLA_WIDTH:, :], preferred_element_type=F32))
    hb = _rms(x1, g_mlp_ref[...]).astype(BF16)
    y_ref[...] = x1
    for s in range(D_FF // FF_SLAB):
        up = jnp.dot(hb, w_up_ref[:, s * FF_SLAB:(s + 1) * FF_SLAB], preferred_element_type=F32)
        up = jnp.square(jnp.maximum(up, 0.0)).astype(BF16)
        y_ref[...] += jnp.dot(up, w_down_ref[s * FF_SLAB:(s + 1) * FF_SLAB, :], preferred_element_type=F32)


def _finish(x, att, gdn, wp):
    nb, n, _ = x.shape
    tm = _row_tile(n, 512)
    row = lambda w: pl.BlockSpec((None, tm, w), lambda b, i: (b, i, 0))
    return pl.pallas_call(
        _finish_kernel,
        grid=(nb, n // tm),
        in_specs=[row(D_MODEL), row(MLA_WIDTH), row(GDN_WIDTH), _const_spec((1, MLA_WIDTH)),
                  _const_spec((D_MODEL, D_MODEL)), _const_spec((1, D_MODEL)),
                  _const_spec((D_MODEL, D_FF)), _const_spec((D_FF, D_MODEL))],
        out_specs=row(D_MODEL),
        out_shape=jax.ShapeDtypeStruct((nb, n, D_MODEL), F32),
        compiler_params=_params("parallel", "parallel"),
        name="finish",
    )(x, att, gdn, wp["g_mla"], wp["w_out"], wp["g_mlp"], wp["w_up"], wp["w_down"])


def _pad_heads(w, head_dim, lane_mask):
    r = w.shape[0]
    w3 = w.reshape(r, MLA_HEADS, head_dim)
    idx = jnp.minimum(_SRC_IDX, head_dim - 1)
    return (jnp.take(w3, idx, axis=2) * lane_mask).reshape(r, MLA_HEADS * LANES)


def _prepare_params(attn_norm, w_in, q_a_norm, w_uq, kv_a_norm, w_uk, w_uv, q_norm, k_norm, mla_out_norm,
                    conv_w, a_log, dt_bias, gdn_out_norm, w_out, mlp_norm, w_up, w_down):
    split = [Q_LORA, Q_LORA + KV_LORA, Q_LORA + KV_LORA + ROPE_DIM]
    split.append(split[-1] + CONV_CH)
    split.append(split[-1] + GDN_WIDTH)
    w_q, w_kv, w_kr, w_u, w_gate, w_logit = jnp.split(w_in, split, axis=1)
    zeros = lambda c: jnp.zeros((D_MODEL, c), w_in.dtype)
    w_kr_placed = jnp.concatenate([w_kr[:, :HALF_ROPE], zeros(64 - HALF_ROPE), w_kr[:, HALF_ROPE:],
                                   zeros(64 - HALF_ROPE)], axis=1)
    w_in_p = jnp.concatenate([w_q, w_kv, w_kr_placed, w_u, w_gate, w_logit,
                              zeros(LANES - 2 * GDN_HEADS)], axis=1)
    lane_pad = (0, LANES - 2 * GDN_HEADS)
    neg_a = jnp.pad(jnp.concatenate([jnp.zeros((GDN_HEADS,), F32), -jnp.exp(a_log.astype(F32))]), lane_pad)
    dtb = jnp.pad(jnp.concatenate([jnp.zeros((GDN_HEADS,), F32), dt_bias.astype(F32)]), lane_pad)
    gdn_const = jnp.zeros((SUBLANES, LANES), F32).at[0].set(neg_a).at[1].set(dtb)
    return {
        "g_attn": attn_norm.reshape(1, D_MODEL).astype(F32),
        "w_in": w_in_p.astype(BF16),
        "g_qa": q_a_norm.reshape(1, Q_LORA).astype(F32),
        "w_uq": _pad_heads(w_uq, QK_HEAD, _SRC_ANY).astype(BF16),
        "g_kv": kv_a_norm.reshape(1, KV_LORA).astype(F32),
        "w_uk": _pad_heads(w_uk, NOPE_DIM, _SRC_NOPE).astype(BF16),
        "w_uv": w_uv.astype(BF16),
        "g_q": jnp.take(q_norm.astype(F32), _SRC_IDX) * _SRC_ANY,
        "g_k": jnp.take(k_norm.astype(F32), _SRC_IDX) * _SRC_ANY,
        "g_mla": mla_out_norm.reshape(1, MLA_WIDTH).astype(F32),
        "conv_w": conv_w.astype(F32),
        "gdn_const": gdn_const,
        "g_gdn": gdn_out_norm.reshape(1, GDN_DV).astype(F32),
        "w_out": w_out.astype(BF16),
        "g_mlp": mlp_norm.reshape(1, D_MODEL).astype(F32),
        "w_up": w_up.astype(BF16),
        "w_down": w_down.astype(BF16),
    }


def _rope_tables(pos, gain, scale):
    inv_freq = ROPE_BASE ** (-jnp.arange(HALF_ROPE, dtype=F32) / HALF_ROPE)
    ang = pos.astype(F32)[:, None] * inv_freq[None, :]
    cos, sin = jnp.cos(ang), jnp.sin(ang)
    n = pos.shape[0]
    ones = lambda c: jnp.ones((n, c), F32)
    zeros = lambda c: jnp.zeros((n, c), F32)
    c_tab = jnp.concatenate([cos, ones(64 - HALF_ROPE), cos, ones(64 - HALF_ROPE)], axis=1)
    s_tab = jnp.concatenate([-sin, zeros(64 - HALF_ROPE), sin, zeros(64 - HALF_ROPE)], axis=1)
    return c_tab * (gain * scale)[None, :], s_tab * (jnp.roll(gain, 64) * scale)[None, :]


def _place_rope_key(kr):
    z = jnp.zeros(kr.shape[:-1] + (64 - HALF_ROPE,), kr.dtype)
    return jnp.concatenate([kr[..., :HALF_ROPE], z, kr[..., HALF_ROPE:], z], axis=-1)


def _unplace_rope_key(krp):
    return jnp.concatenate([krp[..., :HALF_ROPE], krp[..., 64:64 + HALF_ROPE]], axis=-1)


def kernel(x_prompt, x_sample, cache_kv_latent, cache_k_rope, state_gdn, state_conv, meta_tokens, attn_norm,
           w_in, q_a_norm, w_uq, kv_a_norm, w_uk, w_uv, q_norm, k_norm, mla_out_norm, conv_w, a_log, dt_bias,
           gdn_out_norm, w_out, mlp_norm, w_up, w_down):
    assert attn_norm.shape[0] == 1, "single layer"
    wp = _prepare_params(attn_norm[0], w_in[0], q_a_norm[0], w_uq[0], kv_a_norm[0], w_uk[0], w_uv[0],
                         q_norm[0], k_norm[0], mla_out_norm[0], conv_w[0], a_log[0], dt_bias[0],
                         gdn_out_norm[0], w_out[0], mlp_norm[0], w_up[0], w_down[0])
    nb, seq, _ = x_prompt.shape
    db, dseq, _ = x_sample.shape
    n_cache = cache_kv_latent.shape[2]
    past = n_cache - N_META
    q_scale = QK_HEAD ** -0.5 * LOG2_E

    pos_meta = jnp.arange(N_META)
    pos_frames = N_META + jnp.arange(seq)
    pos_cache = N_META + jnp.arange(past)
    pos_new = n_cache + jnp.arange(dseq)
    tabs_q = lambda pos: _rope_tables(pos, wp["g_q"], q_scale)
    tabs_k = lambda pos: _rope_tables(pos, wp["g_k"], 1.0)

    x_meta = meta_tokens.astype(F32)[None]
    _, m_lat, m_krp, m_u, _, m_logit, m_tail = _in_proj(x_meta, *tabs_q(pos_meta), wp)
    m_k, m_v = _kv_expand(m_lat, m_krp, *tabs_k(pos_meta), wp)
    zero_tail = jnp.zeros((1, SUBLANES, CONV_CH), F32)
    m_q, m_kk, m_vv, m_gb, m_gbt = _gdn_prep(m_u, m_logit, zero_tail, wp, N_META)
    zero_state = jnp.zeros((1, GDN_HEADS, GDN_DK, GDN_DV), F32)
    m_gate = jnp.zeros((1, N_META, GDN_WIDTH), BF16)
    _, m_state = _gdn_scan(m_q, m_kk, m_vv, m_gb, m_gbt, m_gate, zero_state, wp, N_META)

    p_q, p_lat, p_krp, p_u, p_gate, p_logit, p_tail = _in_proj(x_prompt, *tabs_q(pos_frames), wp)
    p_k, p_v = _kv_expand(p_lat, p_krp, *tabs_k(pos_frames), wp)
    p_att = _attention(p_q, p_k, p_v, m_k, m_v, causal=True)
    g_q, g_k, g_v, g_gb, g_gbt = _gdn_prep(p_u, p_logit, m_tail, wp, CHUNK)
    p_gdn, p_state = _gdn_scan(g_q, g_k, g_v, g_gb, g_gbt, p_gate, m_state, wp, CHUNK)
    y_prompt = _finish(x_prompt, p_att, p_gdn, wp)

    s_q, s_lat, s_krp, s_u, s_gate, s_logit, s_tail = _in_proj(x_sample, *tabs_q(pos_new), wp)
    s_k, s_v = _kv_expand(s_lat, s_krp, *tabs_k(pos_new), wp)
    c_lat = cache_kv_latent[0].astype(F32)
    c_krp = _place_rope_key(cache_k_rope[0].astype(F32))
    cm_k, cm_v = _kv_expand(c_lat[:, :N_META], c_krp[:, :N_META], *tabs_k(pos_meta), wp)
    cf_k, cf_v = _kv_expand(c_lat[:, N_META:], c_krp[:, N_META:], *tabs_k(pos_cache), wp)
    e_k = jnp.concatenate([cm_k, s_k], axis=1)
    e_v = jnp.concatenate([cm_v, s_v], axis=1)
    s_att = _attention(s_q, cf_k, cf_v, e_k, e_v, causal=False)
    conv0 = jnp.pad(state_conv[0].astype(F32), ((0, 0), (SUBLANES - (CONV_W - 1), 0), (0, 0)))
    d_chunk = CHUNK if dseq % CHUNK == 0 else dseq
    d_q, d_k, d_v, d_gb, d_gbt = _gdn_prep(s_u, s_logit, conv0, wp, d_chunk)
    s_gdn, s_state = _gdn_scan(d_q, d_k, d_v, d_gb, d_gbt, s_gate, state_gdn[0].astype(F32), wp, d_chunk)
    y_sample = _finish(x_sample, s_att, s_gdn, wp)

    tail_rows = slice(SUBLANES - (CONV_W - 1), SUBLANES)
    bcast = lambda a: jnp.broadcast_to(a, (nb,) + a.shape[1:])
    p_kv_latent = jnp.concatenate([bcast(m_lat), p_lat], axis=1)[None]
    p_k_rope = _unplace_rope_key(jnp.concatenate([bcast(m_krp), p_krp], axis=1))[None]
    return (y_prompt, y_sample, p_kv_latent, p_k_rope, p_state[None], p_tail[:, tail_rows][None],
            s_lat[None], _unplace_rope_key(s_krp)[None], s_state[None], s_tail[:, tail_rows][None])
```

```python
import functools
import math

import jax
import jax.numpy as jnp
import numpy as np
from jax import lax
from jax.experimental import pallas as pl
from jax.experimental.pallas import tpu as pltpu

F32 = jnp.float32
BF16 = jnp.bfloat16

D_MODEL = 1024
N_META = 16
CHUNK = 64
EPS = 1e-6
MLA_HEADS = 8
Q_LORA = 384
KV_LORA = 256
NOPE_DIM = 64
ROPE_DIM = 32
QK_HEAD = NOPE_DIM + ROPE_DIM
V_HEAD = 64
MLA_WIDTH = MLA_HEADS * V_HEAD
ROPE_BASE = 10000.0
GDN_HEADS = 4
GDN_DK = 128
GDN_DV = 128
GDN_WIDTH = GDN_HEADS * GDN_DK
CONV_W = 4
CONV_CH = 3 * GDN_WIDTH
D_FF = 4 * D_MODEL

LANES = 128
SUBLANES = 8
QK_WIDTH = MLA_HEADS * LANES
HALF_ROPE = ROPE_DIM // 2
VMEM_LIMIT = 56 * 1024 * 1024
LOG2_E = math.log2(math.e)

COL_Q = 0
COL_KV = COL_Q + Q_LORA
COL_KR = COL_KV + KV_LORA
COL_U = COL_KR + LANES
COL_GATE = COL_U + CONV_CH
COL_LOGIT = COL_GATE + GDN_WIDTH
IN_COLS_PAD = COL_LOGIT + LANES


def _head_lane_source():
    src = [-1] * LANES
    for d in range(HALF_ROPE):
        src[d] = NOPE_DIM + d
        src[64 + d] = NOPE_DIM + HALF_ROPE + d
    for d in range(48):
        src[16 + d] = d
    for d in range(16):
        src[80 + d] = 48 + d
    return src


_SRC = _head_lane_source()
_SRC_IDX = np.array([max(s, 0) for s in _SRC], np.int32)
_SRC_ANY = np.array([1.0 if s >= 0 else 0.0 for s in _SRC], np.float32)
_SRC_NOPE = np.array([1.0 if 0 <= s < NOPE_DIM else 0.0 for s in _SRC], np.float32)


def _mm(a, b):
    return jnp.dot(a.astype(BF16), b.astype(BF16), preferred_element_type=F32)


def _mm_nt(a, b):
    return lax.dot_general(a.astype(BF16), b.astype(BF16), (((1,), (1,)), ((), ())),
                           preferred_element_type=F32)


def _mm_tn(a, b):
    return lax.dot_general(a.astype(BF16), b.astype(BF16), (((0,), (0,)), ((), ())),
                           preferred_element_type=F32)


def _rms(x, g):
    return x * lax.rsqrt(jnp.mean(x * x, axis=-1, keepdims=True) + EPS) * g


def _sigmoid(x):
    return 1.0 / (1.0 + jnp.exp(-x))


def _silu(x):
    half = 0.5 * x
    return half + half * jnp.tanh(half)


def _row_tile(n, cap):
    t = cap
    while t > 1 and n % t:
        t //= 2
    return t


def _params(*sem):
    return pltpu.CompilerParams(dimension_semantics=sem, vmem_limit_bytes=VMEM_LIMIT)


def _const_spec(shape):
    nd = len(shape)
    return pl.BlockSpec(shape, lambda *_: (0,) * nd, pipeline_mode=pl.Buffered(1))


def _head_norm_rope(raw, extra, c_tab, s_tab, out_ref, mxu_sum):
    ones = jnp.ones((LANES, LANES), BF16)
    for h in range(MLA_HEADS):
        seg = raw[:, h * LANES:(h + 1) * LANES]
        if extra is not None:
            seg = seg + extra
        if mxu_sum:
            ss = jnp.dot((seg * seg).astype(BF16), ones, preferred_element_type=F32)
        else:
            ss = jnp.sum(seg * seg, axis=-1, keepdims=True)
        r = lax.rsqrt(ss * (1.0 / QK_HEAD) + EPS)
        out = (seg * c_tab + pltpu.roll(seg, 64, 1) * s_tab) * r
        out_ref[:, h * LANES:(h + 1) * LANES] = out.astype(out_ref.dtype)


def _in_proj_kernel(x_ref, g_attn_ref, w_in_ref, g_qa_ref, w_uq_ref, g_kv_ref, cq_ref, sq_ref,
                    q_ref, lat_ref, krp_ref, u_ref, gate_ref, logit_ref, tail_ref):
    x = x_ref[...]
    hb = _rms(x, g_attn_ref[...]).astype(BF16)

    def proj(lo, hi):
        return jnp.dot(hb, w_in_ref[:, lo:hi], preferred_element_type=F32)

    q_lat = _rms(proj(COL_Q, COL_KV), g_qa_ref[...])
    q_raw = _mm(q_lat, w_uq_ref[...])
    _head_norm_rope(q_raw, None, cq_ref[...], sq_ref[...], q_ref, mxu_sum=False)
    lat_ref[...] = _rms(proj(COL_KV, COL_KR), g_kv_ref[...])
    krp_ref[...] = proj(COL_KR, COL_U)
    u = proj(COL_U, COL_GATE)
    u_ref[...] = u.astype(BF16)
    tail_ref[...] = u[u.shape[0] - SUBLANES:, :]
    gate_ref[...] = proj(COL_GATE, COL_LOGIT).astype(BF16)
    logit_ref[...] = proj(COL_LOGIT, IN_COLS_PAD)


def _in_proj(x, cq, sq, wp):
    nb, n, _ = x.shape
    tm = _row_tile(n, 512)
    grid = (nb, n // tm)
    row = lambda w: pl.BlockSpec((None, tm, w), lambda b, i: (b, i, 0))
    tab = pl.BlockSpec((tm, LANES), lambda b, i: (i, 0))
    out_shape = (
        jax.ShapeDtypeStruct((nb, n, QK_WIDTH), BF16),
        jax.ShapeDtypeStruct((nb, n, KV_LORA), F32),
        jax.ShapeDtypeStruct((nb, n, LANES), F32),
        jax.ShapeDtypeStruct((nb, n, CONV_CH), BF16),
        jax.ShapeDtypeStruct((nb, n, GDN_WIDTH), BF16),
        jax.ShapeDtypeStruct((nb, n, LANES), F32),
        jax.ShapeDtypeStruct((nb, SUBLANES, CONV_CH), F32),
    )
    return pl.pallas_call(
        _in_proj_kernel,
        grid=grid,
        in_specs=[row(D_MODEL), _const_spec((1, D_MODEL)), _const_spec((D_MODEL, IN_COLS_PAD)),
                  _const_spec((1, Q_LORA)), _const_spec((Q_LORA, QK_WIDTH)), _const_spec((1, KV_LORA)),
                  tab, tab],
        out_specs=(row(QK_WIDTH), row(KV_LORA), row(LANES), row(CONV_CH), row(GDN_WIDTH), row(LANES),
                   pl.BlockSpec((None, SUBLANES, CONV_CH), lambda b, i: (b, 0, 0))),
        out_shape=out_shape,
        compiler_params=_params("parallel", "arbitrary"),
        name="in_proj",
    )(x, wp["g_attn"], wp["w_in"], wp["g_qa"], wp["w_uq"], wp["g_kv"], cq, sq)


def _kv_expand_kernel(lat_ref, krp_ref, w_uk_ref, w_uv_ref, ck_ref, sk_ref, k_ref, v_ref):
    latb = lat_ref[...].astype(BF16)
    k_raw = jnp.dot(latb, w_uk_ref[...], preferred_element_type=F32)
    _head_norm_rope(k_raw, krp_ref[...], ck_ref[...], sk_ref[...], k_ref, mxu_sum=True)
    v_ref[...] = jnp.dot(latb, w_uv_ref[...], preferred_element_type=F32).astype(BF16)


def _kv_expand(lat, krp, ck, sk, wp):
    nb, n, _ = lat.shape
    tm = _row_tile(n, 512)
    row = lambda w: pl.BlockSpec((None, tm, w), lambda b, i: (b, i, 0))
    tab = pl.BlockSpec((tm, LANES), lambda b, i: (i, 0))
    return pl.pallas_call(
        _kv_expand_kernel,
        grid=(nb, n // tm),
        in_specs=[row(KV_LORA), row(LANES), _const_spec((KV_LORA, QK_WIDTH)),
                  _const_spec((KV_LORA, MLA_WIDTH)), tab, tab],
        out_specs=(row(QK_WIDTH), row(MLA_WIDTH)),
        out_shape=(jax.ShapeDtypeStruct((nb, n, QK_WIDTH), BF16),
                   jax.ShapeDtypeStruct((nb, n, MLA_WIDTH), BF16)),
        compiler_params=_params("parallel", "parallel"),
        name="kv_expand",
    )(lat, krp, wp["w_uk"], wp["w_uv"], ck, sk)


def _lane_tile(m, n):
    if n % LANES:
        return m[:, :n]
    return m if n == LANES else jnp.concatenate([m] * (n // LANES), axis=1)


def _online_softmax_step(s_pieces, v_pieces, m_old, acc_old):
    heads = range(len(s_pieces))
    row_max = []
    for h in heads:
        r = jnp.max(s_pieces[h][0], axis=-1, keepdims=True)
        for piece in s_pieces[h][1:]:
            r = jnp.maximum(r, jnp.max(piece, axis=-1, keepdims=True))
        row_max.append(r)
    if m_old is None:
        m_new = [jnp.broadcast_to(row_max[h], (row_max[h].shape[0], LANES)) for h in heads]
    else:
        m_new = [jnp.maximum(m_old[h], row_max[h]) for h in heads]
    probs = [[jnp.exp2(piece - _lane_tile(m_new[h], piece.shape[1])).astype(BF16) for piece in s_pieces[h]]
             for h in heads]
    pv = []
    for h in heads:
        t = jnp.dot(probs[h][0], v_pieces[h][0], preferred_element_type=F32)
        for p, v in zip(probs[h][1:], v_pieces[h][1:]):
            t = t + jnp.dot(p, v, preferred_element_type=F32)
        pv.append(t)
    if acc_old is None:
        return m_new, pv
    return m_new, [jnp.exp2(m_old[h] - m_new[h]) * acc_old[h] + pv[h] for h in heads]


def _attention_kernel(q_ref, k_ref, v_ref, ke_ref, ve_ref, o_ref, m_sc, acc_sc, *, tq, tk, causal):
    i = pl.program_id(1)
    n_kv = k_ref.shape[0] // tk
    heads = range(MLA_HEADS)
    low_half = lax.broadcasted_iota(jnp.int32, (1, LANES), 1) < V_HEAD
    one = jnp.ones((), BF16)

    def q_of(h):
        return q_ref[:, h * LANES:(h + 1) * LANES]

    def values(ref, rows):
        out = []
        for pair in range(MLA_HEADS // 2):
            vp = ref[rows, pair * LANES:(pair + 1) * LANES]
            out += [jnp.where(low_half, vp, one), jnp.where(low_half, one, vp)]
        return out

    def scores(ref, rows):
        return [_mm_nt(q_of(h), ref[rows, h * LANES:(h + 1) * LANES]) for h in heads]

    s_first = [[s] for s in scores(ke_ref, slice(None))]
    v_first = [[v] for v in values(ve_ref, slice(None))]
    ratio = tq // tk
    if causal:
        qc = lax.broadcasted_iota(jnp.int32, (tq, tk), 0) // CHUNK
        for d in range(ratio):
            diag = pl.ds(pl.multiple_of((i * ratio + d) * tk, tk), tk)
            kc = (lax.broadcasted_iota(jnp.int32, (tq, tk), 1) + d * tk) // CHUNK
            visible = kc <= qc
            s_diag = scores(k_ref, diag)
            v_diag = values(v_ref, diag)
            for h in heads:
                s_first[h].append(jnp.where(visible, s_diag[h], -jnp.inf))
                v_first[h].append(v_diag[h])
    m0, acc0 = _online_softmax_step(s_first, v_first, None, None)
    for h in heads:
        m_sc[h] = m0[h]
        acc_sc[h] = acc0[h]

    def full_tiles(first, count):
        tiles = [pl.ds(pl.multiple_of((first + t) * tk, tk), tk) for t in range(count)]
        m_old = [m_sc[h] for h in heads]
        acc_old = [acc_sc[h] for h in heads]
        s_tiles = [scores(k_ref, rows) for rows in tiles]
        v_tiles = [values(v_ref, rows) for rows in tiles]
        s = [[s_tiles[t][h] for t in range(count)] for h in heads]
        v = [[v_tiles[t][h] for t in range(count)] for h in heads]
        m_new, acc_new = _online_softmax_step(s, v, m_old, acc_old)
        for h in heads:
            m_sc[h] = m_new[h]
            acc_sc[h] = acc_new[h]

    def tile_pair(j, carry):
        full_tiles(2 * j, 2)
        return carry

    if causal:
        n_full = i * ratio
        lax.fori_loop(0, n_full // 2, tile_pair, 0)
        if ratio % 2:
            @pl.when(n_full % 2 == 1)
            def _():
                full_tiles(n_full - 1, 1)
    else:
        lax.fori_loop(0, n_kv // 2, tile_pair, 0)
        if n_kv % 2:
            full_tiles(n_kv - 1, 1)

    for pair in range(MLA_HEADS // 2):
        even = acc_sc[2 * pair]
        odd = acc_sc[2 * pair + 1]
        o = jnp.where(low_half, even / pltpu.roll(even, V_HEAD, 1), odd / pltpu.roll(odd, V_HEAD, 1))
        o_ref[:, pair * LANES:(pair + 1) * LANES] = o.astype(o_ref.dtype)


def _attention(q, k, v, k_extra, v_extra, causal):
    nb, nq, _ = q.shape
    nk = k.shape[1]
    ne = k_extra.shape[1]
    shared_extra = k_extra.shape[0] == 1
    tk = _row_tile(nk, 256)
    tq = _row_tile(nq, 2 * tk) if causal else _row_tile(nq, 256)
    eb = (lambda b: 0) if shared_extra else (lambda b: b)
    kernel = functools.partial(_attention_kernel, tq=tq, tk=tk, causal=causal)
    return pl.pallas_call(
        kernel,
        grid=(nb, nq // tq),
        in_specs=[pl.BlockSpec((None, tq, QK_WIDTH), lambda b, i: (b, i, 0)),
                  pl.BlockSpec((None, nk, QK_WIDTH), lambda b, i: (b, 0, 0)),
                  pl.BlockSpec((None, nk, MLA_WIDTH), lambda b, i: (b, 0, 0)),
                  pl.BlockSpec((None, ne, QK_WIDTH), lambda b, i: (eb(b), 0, 0)),
                  pl.BlockSpec((None, ne, MLA_WIDTH), lambda b, i: (eb(b), 0, 0))],
        out_specs=pl.BlockSpec((None, tq, MLA_WIDTH), lambda b, i: (b, i, 0)),
        out_shape=jax.ShapeDtypeStruct((nb, nq, MLA_WIDTH), BF16),
        scratch_shapes=[pltpu.VMEM((MLA_HEADS, tq, LANES), F32),
                        pltpu.VMEM((MLA_HEADS, tq, LANES), F32)],
        compiler_params=_params("parallel", "arbitrary"),
        name="attention",
    )(q, k, v, k_extra, v_extra)


def _gdn_prep_kernel(u_ref, uprev_ref, init_ref, w_ref, logit_ref, const_ref,
                     q_ref, k_ref, v_ref, gb_ref, gbt_ref, ext_sc, *, chunk):
    tm = u_ref.shape[0]
    i = pl.program_id(1)
    prev = jnp.where(i == 0, init_ref[...], uprev_ref[...].astype(F32))
    ext_sc[0:SUBLANES, :] = prev
    ext_sc[SUBLANES:SUBLANES + tm, :] = u_ref[...].astype(F32)
    y = w_ref[CONV_W - 1:CONV_W, :] * ext_sc[SUBLANES:SUBLANES + tm, :]
    for j in range(CONV_W - 1):
        off = SUBLANES - (CONV_W - 1) + j
        y = y + w_ref[j:j + 1, :] * ext_sc[off:off + tm, :]
    y = _silu(y)
    for h in range(GDN_HEADS):
        lo = h * GDN_DK
        qh = y[:, lo:lo + GDN_DK]
        kh = y[:, GDN_WIDTH + lo:GDN_WIDTH + lo + GDN_DK]
        q_ref[:, lo:lo + GDN_DK] = (
            qh * (lax.rsqrt(jnp.sum(qh * qh, axis=-1, keepdims=True) + EPS) * GDN_DK ** -0.5)).astype(BF16)
        k_ref[:, lo:lo + GDN_DK] = (
            kh * lax.rsqrt(jnp.sum(kh * kh, axis=-1, keepdims=True) + EPS)).astype(BF16)
    v_ref[...] = y[:, 2 * GDN_WIDTH:].astype(BF16)

    lg = logit_ref[...]
    beta = _sigmoid(lg)
    z = lg + const_ref[1:2, :]
    softplus = jnp.maximum(z, 0.0) + jnp.log(1.0 + jnp.exp(-jnp.abs(z)))
    g = const_ref[0:1, :] * softplus
    row_in_chunk = lax.broadcasted_iota(jnp.int32, g.shape, 0) % chunk
    step = 1
    while step < chunk:
        g = g + jnp.where(row_in_chunk >= step, pltpu.roll(g, step, 0), 0.0)
        step *= 2
    lane = lax.broadcasted_iota(jnp.int32, g.shape, 1)
    gb = jnp.where(lane < GDN_HEADS, beta, g)
    gb_ref[...] = gb
    sel = (lax.broadcasted_iota(jnp.int32, (SUBLANES, LANES), 0)
           == lax.broadcasted_iota(jnp.int32, (SUBLANES, LANES), 1)).astype(F32)
    gbt_ref[...] = lax.dot_general(sel, gb, (((1,), (1,)), ((), ())), precision=lax.Precision.HIGHEST,
                                   preferred_element_type=F32)


def _gdn_prep(u, logits, init, wp, chunk):
    nb, n, _ = u.shape
    tm = _row_tile(n, 512)
    per = tm // SUBLANES
    shared_init = init.shape[0] == 1
    ib = (lambda b: 0) if shared_init else (lambda b: b)
    row = lambda w: pl.BlockSpec((None, tm, w), lambda b, i: (b, i, 0))
    kernel = functools.partial(_gdn_prep_kernel, chunk=chunk)
    return pl.pallas_call(
        kernel,
        grid=(nb, n // tm),
        in_specs=[row(CONV_CH),
                  pl.BlockSpec((None, SUBLANES, CONV_CH), lambda b, i: (b, jnp.maximum(i * per - 1, 0), 0)),
                  pl.BlockSpec((None, SUBLANES, CONV_CH), lambda b, i: (ib(b), 0, 0)),
                  _const_spec((CONV_W, CONV_CH)), row(LANES), _const_spec((SUBLANES, LANES))],
        out_specs=(row(GDN_WIDTH), row(GDN_WIDTH), row(GDN_WIDTH), row(LANES),
                   pl.BlockSpec((None, SUBLANES, tm), lambda b, i: (b, 0, i))),
        out_shape=(jax.ShapeDtypeStruct((nb, n, GDN_WIDTH), BF16),
                   jax.ShapeDtypeStruct((nb, n, GDN_WIDTH), BF16),
                   jax.ShapeDtypeStruct((nb, n, GDN_WIDTH), BF16),
                   jax.ShapeDtypeStruct((nb, n, LANES), F32),
                   jax.ShapeDtypeStruct((nb, SUBLANES, n), F32)),
        scratch_shapes=[pltpu.VMEM((tm + SUBLANES, CONV_CH), F32)],
        compiler_params=_params("parallel", "parallel"),
        name="gdn_prep",
    )(u, u, init, wp["conv_w"], logits, wp["gdn_const"])


def _unit_lower_inverses(a_list):
    c = a_list[0].shape[0]
    items = range(len(a_list))
    eye = (lax.broadcasted_iota(jnp.int32, (c, c), 0) == lax.broadcasted_iota(jnp.int32, (c, c), 1)).astype(F32)
    p = [eye - a for a in a_list]
    power = [_mm(a, a) for a in a_list]
    span = 2
    while span < c:
        if 2 * span >= c:
            p = [p[n] + _mm(p[n], power[n]) for n in items]
        else:
            both = [_mm(jnp.concatenate([p[n], power[n]], axis=0), power[n]) for n in items]
            p = [p[n] + both[n][:c] for n in items]
            power = [both[n][c:] for n in items]
        span *= 2
    return p


def _gdn_scan_kernel(q_ref, k_ref, v_ref, gb_ref, gbt_ref, gate_ref, s0_ref, gnorm_ref,
                     o_ref, s_out_ref, s_sc, *, chunk):
    i = pl.program_id(1)
    tm = q_ref.shape[0]
    n_chunks = tm // chunk
    heads = range(GDN_HEADS)

    @pl.when(i == 0)
    def _():
        s_sc[...] = s0_ref[...]

    r = lax.broadcasted_iota(jnp.int32, (chunk, chunk), 0)
    c = lax.broadcasted_iota(jnp.int32, (chunk, chunk), 1)
    causal = r >= c
    strict = r > c

    items = [(ci, h) for ci in range(n_chunks) for h in heads]
    rows_of = lambda ci: slice(ci * chunk, (ci + 1) * chunk)
    cols_of = lambda h: slice(h * GDN_DK, (h + 1) * GDN_DK)
    qh = [q_ref[rows_of(ci), cols_of(h)] for ci, h in items]
    kh = [k_ref[rows_of(ci), cols_of(h)] for ci, h in items]
    kf = [k.astype(F32) for k in kh]
    vf = [v_ref[rows_of(ci), cols_of(h)].astype(F32) for ci, h in items]
    beta = [gb_ref[rows_of(ci), h:h + 1] for ci, h in items]
    gc = [gb_ref[rows_of(ci), GDN_HEADS + h:GDN_HEADS + h + 1] for ci, h in items]
    gc_row = [gbt_ref[GDN_HEADS + h:GDN_HEADS + h + 1, rows_of(ci)] for ci, h in items]
    n_items = range(len(items))
    g_last = [gc[n][chunk - 1:chunk, :] for n in n_items]
    decay = [jnp.exp(jnp.where(causal, gc[n] - gc_row[n], -jnp.inf)) for n in n_items]
    e_gc = [jnp.exp(gc[n]) for n in n_items]
    kb = [kf[n] * beta[n] for n in n_items]
    kq = [_mm_nt(jnp.concatenate([kb[n].astype(BF16), qh[n]], axis=0), kh[n]) for n in n_items]
    a = [jnp.where(strict, kq[n][:chunk] * decay[n], 0.0) for n in n_items]
    qk = [(kq[n][chunk:] * decay[n]).astype(BF16) for n in n_items]
    t_inv = _unit_lower_inverses(a)
    uw = [_mm(t_inv[n], jnp.concatenate([vf[n] * beta[n], kb[n] * e_gc[n]], axis=1)) for n in n_items]
    wq = [jnp.concatenate([uw[n][:, GDN_DV:], qh[n].astype(F32) * e_gc[n]], axis=0).astype(BF16)
          for n in n_items]
    k_tail = [(kf[n] * jnp.exp(g_last[n] - gc[n])).astype(BF16) for n in n_items]
    s_decay = [jnp.exp(g_last[n]) for n in n_items]

    gnorm = gnorm_ref[...]
    state = [s_sc[h] for h in heads]
    for ci in range(n_chunks):
        idx = [ci * GDN_HEADS + h for h in heads]
        ws = [_mm(wq[n], state[h]) for h, n in zip(heads, idx)]
        v_new = [(uw[n][:, :GDN_DV] - ws[h][:chunk]).astype(BF16) for h, n in zip(heads, idx)]
        o = [ws[h][chunk:] + jnp.dot(qk[n], v_new[h], preferred_element_type=F32) for h, n in zip(heads, idx)]
        state = [state[h] * s_decay[n] + _mm_tn(k_tail[n], v_new[h]) for h, n in zip(heads, idx)]
        for h in heads:
            gate = gate_ref[rows_of(ci), cols_of(h)].astype(F32)
            o_ref[rows_of(ci), cols_of(h)] = (_rms(o[h], gnorm) * _silu(gate)).astype(o_ref.dtype)
    for h in heads:
        s_sc[h] = state[h]

    @pl.when(i == pl.num_programs(1) - 1)
    def _():
        s_out_ref[...] = s_sc[...]


def _gdn_scan(q, k, v, gb, gbt, gate, s0, wp, chunk):
    nb, n, _ = q.shape
    tm = _row_tile(n, 512)
    shared_s0 = s0.shape[0] == 1
    sb = (lambda b: 0) if shared_s0 else (lambda b: b)
    row = lambda w: pl.BlockSpec((None, tm, w), lambda b, i: (b, i, 0))
    state = (None, GDN_HEADS, GDN_DK, GDN_DV)
    kernel = functools.partial(_gdn_scan_kernel, chunk=chunk)
    return pl.pallas_call(
        kernel,
        grid=(nb, n // tm),
        in_specs=[row(GDN_WIDTH), row(GDN_WIDTH), row(GDN_WIDTH), row(LANES),
                  pl.BlockSpec((None, SUBLANES, tm), lambda b, i: (b, 0, i)),
                  row(GDN_WIDTH),
                  pl.BlockSpec(state, lambda b, i: (sb(b), 0, 0, 0)),
                  _const_spec((1, GDN_DV))],
        out_specs=(row(GDN_WIDTH), pl.BlockSpec(state, lambda b, i: (b, 0, 0, 0))),
        out_shape=(jax.ShapeDtypeStruct((nb, n, GDN_WIDTH), BF16),
                   jax.ShapeDtypeStruct((nb, GDN_HEADS, GDN_DK, GDN_DV), F32)),
        scratch_shapes=[pltpu.VMEM((GDN_HEADS, GDN_DK, GDN_DV), F32)],
        compiler_params=_params("parallel", "arbitrary"),
        name="gdn_scan",
    )(q, k, v, gb, gbt, gate, s0, wp["g_gdn"])


FF_SLAB = 1024


def _finish_kernel(x_ref, att_ref, gdn_ref, g_mla_ref, w_out_ref, g_mlp_ref, w_up_ref, w_down_ref, y_ref):
    att = _rms(att_ref[...].astype(F32), g_mla_ref[...])
    x1 = (x_ref[...] + _mm(att, w_out_ref[0:MLA_WIDTH, :])
          + jnp.dot(gdn_ref[...], w_out_ref[MLA_WIDTH:, :], preferred_element_type=F32))
    hb = _rms(x1, g_mlp_ref[...]).astype(BF16)
    y_ref[...] = x1
    for s in range(D_FF // FF_SLAB):
        up = jnp.dot(hb, w_up_ref[:, s * FF_SLAB:(s + 1) * FF_SLAB], preferred_element_type=F32)
        up = jnp.square(jnp.maximum(up, 0.0)).astype(BF16)
        y_ref[...] += jnp.dot(up, w_down_ref[s * FF_SLAB:(s + 1) * FF_SLAB, :], preferred_element_type=F32)


def _finish(x, att, gdn, wp):
    nb, n, _ = x.shape
    tm = _row_tile(n, 512)
    row = lambda w: pl.BlockSpec((None, tm, w), lambda b, i: (b, i, 0))
    return pl.pallas_call(
        _finish_kernel,
        grid=(nb, n // tm),
        in_specs=[row(D_MODEL), row(MLA_WIDTH), row(GDN_WIDTH), _const_spec((1, MLA_WIDTH)),
                  _const_spec((D_MODEL, D_MODEL)), _const_spec((1, D_MODEL)),
                  _const_spec((D_MODEL, D_FF)), _const_spec((D_FF, D_MODEL))],
        out_specs=row(D_MODEL),
        out_shape=jax.ShapeDtypeStruct((nb, n, D_MODEL), F32),
        compiler_params=_params("parallel", "parallel"),
        name="finish",
    )(x, att, gdn, wp["g_mla"], wp["w_out"], wp["g_mlp"], wp["w_up"], wp["w_down"])


def _pad_heads(w, head_dim, lane_mask):
    r = w.shape[0]
    w3 = w.reshape(r, MLA_HEADS, head_dim)
    idx = jnp.minimum(_SRC_IDX, head_dim - 1)
    return (jnp.take(w3, idx, axis=2) * lane_mask).reshape(r, MLA_HEADS * LANES)


def _prepare_params(attn_norm, w_in, q_a_norm, w_uq, kv_a_norm, w_uk, w_uv, q_norm, k_norm, mla_out_norm,
                    conv_w, a_log, dt_bias, gdn_out_norm, w_out, mlp_norm, w_up, w_down):
    split = [Q_LORA, Q_LORA + KV_LORA, Q_LORA + KV_LORA + ROPE_DIM]
    split.append(split[-1] + CONV_CH)
    split.append(split[-1] + GDN_WIDTH)
    w_q, w_kv, w_kr, w_u, w_gate, w_logit = jnp.split(w_in, split, axis=1)
    zeros = lambda c: jnp.zeros((D_MODEL, c), w_in.dtype)
    w_kr_placed = jnp.concatenate([w_kr[:, :HALF_ROPE], zeros(64 - HALF_ROPE), w_kr[:, HALF_ROPE:],
                                   zeros(64 - HALF_ROPE)], axis=1)
    w_in_p = jnp.concatenate([w_q, w_kv, w_kr_placed, w_u, w_gate, w_logit,
                              zeros(LANES - 2 * GDN_HEADS)], axis=1)
    lane_pad = (0, LANES - 2 * GDN_HEADS)
    neg_a = jnp.pad(jnp.concatenate([jnp.zeros((GDN_HEADS,), F32), -jnp.exp(a_log.astype(F32))]), lane_pad)
    dtb = jnp.pad(jnp.concatenate([jnp.zeros((GDN_HEADS,), F32), dt_bias.astype(F32)]), lane_pad)
    gdn_const = jnp.zeros((SUBLANES, LANES), F32).at[0].set(neg_a).at[1].set(dtb)
    return {
        "g_attn": attn_norm.reshape(1, D_MODEL).astype(F32),
        "w_in": w_in_p.astype(BF16),
        "g_qa": q_a_norm.reshape(1, Q_LORA).astype(F32),
        "w_uq": _pad_heads(w_uq, QK_HEAD, _SRC_ANY).astype(BF16),
        "g_kv": kv_a_norm.reshape(1, KV_LORA).astype(F32),
        "w_uk": _pad_heads(w_uk, NOPE_DIM, _SRC_NOPE).astype(BF16),
        "w_uv": w_uv.astype(BF16),
        "g_q": jnp.take(q_norm.astype(F32), _SRC_IDX) * _SRC_ANY,
        "g_k": jnp.take(k_norm.astype(F32), _SRC_IDX) * _SRC_ANY,
        "g_mla": mla_out_norm.reshape(1, MLA_WIDTH).astype(F32),
        "conv_w": conv_w.astype(F32),
        "gdn_const": gdn_const,
        "g_gdn": gdn_out_norm.reshape(1, GDN_DV).astype(F32),
        "w_out": w_out.astype(BF16),
        "g_mlp": mlp_norm.reshape(1, D_MODEL).astype(F32),
        "w_up": w_up.astype(BF16),
        "w_down": w_down.astype(BF16),
    }


def _rope_tables(pos, gain, scale):
    inv_freq = ROPE_BASE ** (-jnp.arange(HALF_ROPE, dtype=F32) / HALF_ROPE)
    ang = pos.astype(F32)[:, None] * inv_freq[None, :]
    cos, sin = jnp.cos(ang), jnp.sin(ang)
    n = pos.shape[0]
    ones = lambda c: jnp.ones((n, c), F32)
    zeros = lambda c: jnp.zeros((n, c), F32)
    c_tab = jnp.concatenate([cos, ones(64 - HALF_ROPE), cos, ones(64 - HALF_ROPE)], axis=1)
    s_tab = jnp.concatenate([-sin, zeros(64 - HALF_ROPE), sin, zeros(64 - HALF_ROPE)], axis=1)
    return c_tab * (gain * scale)[None, :], s_tab * (jnp.roll(gain, 64) * scale)[None, :]


def _place_rope_key(kr):
    z = jnp.zeros(kr.shape[:-1] + (64 - HALF_ROPE,), kr.dtype)
    return jnp.concatenate([kr[..., :HALF_ROPE], z, kr[..., HALF_ROPE:], z], axis=-1)


def _unplace_rope_key(krp):
    return jnp.concatenate([krp[..., :HALF_ROPE], krp[..., 64:64 + HALF_ROPE]], axis=-1)


def kernel(x_prompt, x_sample, cache_kv_latent, cache_k_rope, state_gdn, state_conv, meta_tokens, attn_norm,
           w_in, q_a_norm, w_uq, kv_a_norm, w_uk, w_uv, q_norm, k_norm, mla_out_norm, conv_w, a_log, dt_bias,
           gdn_out_norm, w_out, mlp_norm, w_up, w_down):
    assert attn_norm.shape[0] == 1, "single layer"
    wp = _prepare_params(attn_norm[0], w_in[0], q_a_norm[0], w_uq[0], kv_a_norm[0], w_uk[0], w_uv[0],
                         q_norm[0], k_norm[0], mla_out_norm[0], conv_w[0], a_log[0], dt_bias[0],
                         gdn_out_norm[0], w_out[0], mlp_norm[0], w_up[0], w_down[0])
    nb, seq, _ = x_prompt.shape
    db, dseq, _ = x_sample.shape
    n_cache = cache_kv_latent.shape[2]
    past = n_cache - N_META
    q_scale = QK_HEAD ** -0.5 * LOG2_E

    pos_meta = jnp.arange(N_META)
    pos_frames = N_META + jnp.arange(seq)
    pos_cache = N_META + jnp.arange(past)
    pos_new = n_cache + jnp.arange(dseq)
    tabs_q = lambda pos: _rope_tables(pos, wp["g_q"], q_scale)
    tabs_k = lambda pos: _rope_tables(pos, wp["g_k"], 1.0)

    x_meta = meta_tokens.astype(F32)[None]
    _, m_lat, m_krp, m_u, _, m_logit, m_tail = _in_proj(x_meta, *tabs_q(pos_meta), wp)
    m_k, m_v = _kv_expand(m_lat, m_krp, *tabs_k(pos_meta), wp)
    zero_tail = jnp.zeros((1, SUBLANES, CONV_CH), F32)
    m_q, m_kk, m_vv, m_gb, m_gbt = _gdn_prep(m_u, m_logit, zero_tail, wp, N_META)
    zero_state = jnp.zeros((1, GDN_HEADS, GDN_DK, GDN_DV), F32)
    m_gate = jnp.zeros((1, N_META, GDN_WIDTH), BF16)
    _, m_state = _gdn_scan(m_q, m_kk, m_vv, m_gb, m_gbt, m_gate, zero_state, wp, N_META)

    p_q, p_lat, p_krp, p_u, p_gate, p_logit, p_tail = _in_proj(x_prompt, *tabs_q(pos_frames), wp)
    p_k, p_v = _kv_expand(p_lat, p_krp, *tabs_k(pos_frames), wp)
    p_att = _attention(p_q, p_k, p_v, m_k, m_v, causal=True)
    g_q, g_k, g_v, g_gb, g_gbt = _gdn_prep(p_u, p_logit, m_tail, wp, CHUNK)
    p_gdn, p_state = _gdn_scan(g_q, g_k, g_v, g_gb, g_gbt, p_gate, m_state, wp, CHUNK)
    y_prompt = _finish(x_prompt, p_att, p_gdn, wp)

    s_q, s_lat, s_krp, s_u, s_gate, s_logit, s_tail = _in_proj(x_sample, *tabs_q(pos_new), wp)
    s_k, s_v = _kv_expand(s_lat, s_krp, *tabs_k(pos_new), wp)
    c_lat = cache_kv_latent[0].astype(F32)
    c_krp = _place_rope_key(cache_k_rope[0].astype(F32))
    cm_k, cm_v = _kv_expand(c_lat[:, :N_META], c_krp[:, :N_META], *tabs_k(pos_meta), wp)
    cf_k, cf_v = _kv_expand(c_lat[:, N_META:], c_krp[:, N_META:], *tabs_k(pos_cache), wp)
    e_k = jnp.concatenate([cm_k, s_k], axis=1)
    e_v = jnp.concatenate([cm_v, s_v], axis=1)
    s_att = _attention(s_q, cf_k, cf_v, e_k, e_v, causal=False)
    conv0 = jnp.pad(state_conv[0].astype(F32), ((0, 0), (SUBLANES - (CONV_W - 1), 0), (0, 0)))
    d_chunk = CHUNK if dseq % CHUNK == 0 else dseq
    d_q, d_k, d_v, d_gb, d_gbt = _gdn_prep(s_u, s_logit, conv0, wp, d_chunk)
    s_gdn, s_state = _gdn_scan(d_q, d_k, d_v, d_gb, d_gbt, s_gate, state_gdn[0].astype(F32), wp, d_chunk)
    y_sample = _finish(x_sample, s_att, s_gdn, wp)

    tail_rows = slice(SUBLANES - (CONV_W - 1), SUBLANES)
    bcast = lambda a: jnp.broadcast_to(a, (nb,) + a.shape[1:])
    p_kv_latent = jnp.concatenate([bcast(m_lat), p_lat], axis=1)[None]
    p_k_rope = _unplace_rope_key(jnp.concatenate([bcast(m_krp), p_krp], axis=1))[None]
    return (y_prompt, y_sample, p_kv_latent, p_k_rope, p_state[None], p_tail[:, tail_rows][None],
            s_lat[None], _unplace_rope_key(s_krp)[None], s_state[None], s_tail[:, tail_rows][None])
```
